```python
import jax, jax.numpy as jnp
from jax import lax
import numpy as np

D_MODEL = 1024
BATCH = 4
SEQ = 8192
DEPTH = 1

HEAD_DIM = 64
DIFF_HEADS = 4
DIFF_V_DIM = 2 * HEAD_DIM
SB_HEADS = 8
D_DIFF = DIFF_HEADS * DIFF_V_DIM
D_SB = SB_HEADS * HEAD_DIM
D_MIX = D_DIFF + D_SB
D_FF = 2816
BLOCK_Q = 128
N_MOD = 9
RMS_EPS = 1e-6
FFN_RES_WEIGHT = 0.5
MIX_RES_WEIGHT = 1.0

kernel_name = "hybrid_diffattn_stickbreaking_macaron"


def rms_norm(x, gain):
    xf = x.astype(jnp.float32)
    y = xf * lax.rsqrt(jnp.mean(xf * xf, axis=-1, keepdims=True) + RMS_EPS)
    return (y * gain.astype(jnp.float32)).astype(x.dtype)


def swiglu(h, w_gate, w_up, w_down):
    return (jax.nn.silu(h @ w_gate) * (h @ w_up)) @ w_down


def sandwich(x, f, g_pre, g_post, shift, scale, gate, res_w):
    h = rms_norm(x, g_pre) * (1.0 + scale[:, None, :]) + shift[:, None, :]
    y = rms_norm(f(h), g_post)
    return x + res_w * gate[:, None, :] * y


def alibi_slopes(n):
    return 2.0 ** (-8.0 * jnp.arange(1, n + 1, dtype=jnp.float32) / n)


def hybrid_mixer(h, w_in, w_out, lam_q1, lam_k1, lam_q2, lam_k2, diff_subln, sb_beta, lambda_init):
    B, S, _ = h.shape
    nb = S // BLOCK_Q
    scale = 1.0 / np.sqrt(HEAD_DIM).astype(np.float32)
    proj = h @ w_in
    dq, dk, dv, sq, sk, sv = jnp.split(proj, 6, axis=-1)
    dq = dq.reshape(B, S, DIFF_HEADS, 2, HEAD_DIM)
    dk = dk.reshape(B, S, DIFF_HEADS, 2, HEAD_DIM)
    dv = dv.reshape(B, S, DIFF_HEADS, DIFF_V_DIM)
    sq = sq.reshape(B, S, SB_HEADS, HEAD_DIM)
    sk = sk.reshape(B, S, SB_HEADS, HEAD_DIM)
    sv = sv.reshape(B, S, SB_HEADS, HEAD_DIM)

    f32 = jnp.float32
    lam = (jnp.exp(jnp.sum(lam_q1.astype(f32) * lam_k1.astype(f32)))
           - jnp.exp(jnp.sum(lam_q2.astype(f32) * lam_k2.astype(f32))) + lambda_init)
    slopes = alibi_slopes(DIFF_HEADS)
    key_pos = jnp.arange(S)

    dq_blk = dq.reshape(B, nb, BLOCK_Q, DIFF_HEADS, 2, HEAD_DIM).transpose(1, 0, 2, 3, 4, 5)
    sq_blk = sq.reshape(B, nb, BLOCK_Q, SB_HEADS, HEAD_DIM).transpose(1, 0, 2, 3, 4)

    def block(args):
        i, q_d, q_s = args
        q_pos = i * BLOCK_Q + jnp.arange(BLOCK_Q)
        dist = q_pos[:, None] - key_pos[None, :]
        s_d = jnp.einsum('bqhcd,bkhcd->bhcqk', q_d, dk).astype(f32) * scale
        s_d = s_d - slopes[None, :, None, None, None] * dist.astype(f32)
        s_d = jnp.where(dist >= 0, s_d, -jnp.inf)
        p = jax.nn.softmax(s_d, axis=-1)
        w_diff = p[:, :, 0] - lam * p[:, :, 1]
        o_d = jnp.einsum('bhqk,bkhe->bqhe', w_diff.astype(dv.dtype), dv)
        z = jnp.einsum('bqhd,bkhd->bhqk', q_s, sk).astype(f32) * scale
        strict = dist > 0
        log_beta = jax.nn.log_sigmoid(z)
        log_1m = jnp.where(strict, jax.nn.log_sigmoid(-z), 0.0)
        suffix = lax.cumsum(log_1m, axis=3, reverse=True) - log_1m
        att = jnp.where(strict, jnp.exp(log_beta + suffix), 0.0)
        o_s = jnp.einsum('bhqk,bkhd->bqhd', att.astype(sv.dtype), sv)
        return o_d, o_s

    o_d, o_s = lax.map(block, (jnp.arange(nb), dq_blk, sq_blk))
    o_d = o_d.transpose(1, 0, 2, 3, 4).reshape(B, S, DIFF_HEADS, DIFF_V_DIM)
    o_s = o_s.transpose(1, 0, 2, 3, 4).reshape(B, S, D_SB)
    o_d = (rms_norm(o_d, diff_subln) * (1.0 - lambda_init)).reshape(B, S, D_DIFF)
    o_s = rms_norm(o_s, sb_beta)
    return jnp.concatenate([o_d, o_s], axis=-1) @ w_out


def setup_inputs(seed: int = 0) -> dict:
    key = jax.random.key(seed)
    ks = jax.random.split(key, 24)
    f32 = jnp.float32
    nrm = lambda k, shape, s: jax.random.normal(k, shape, f32) * s
    gain = lambda k, n: 1.0 + nrm(k, (DEPTH, n), 0.02)
    D = D_MODEL
    return {
        "x": nrm(ks[0], (BATCH, SEQ, D), 1.0),
        "c": nrm(ks[1], (BATCH, D), 1.0),
        "w_ada": nrm(ks[2], (DEPTH, D, N_MOD * D), 0.5 * D ** -0.5),
        "b_ada": nrm(ks[3], (DEPTH, N_MOD * D), 0.01),
        "ffn1_g_pre": gain(ks[4], D),
        "ffn1_g_post": gain(ks[5], D),
        "ffn1_w_gate": nrm(ks[6], (DEPTH, D, D_FF), D ** -0.5),
        "ffn1_w_up": nrm(ks[7], (DEPTH, D, D_FF), D ** -0.5),
        "ffn1_w_down": nrm(ks[8], (DEPTH, D_FF, D), D_FF ** -0.5),
        "mix_g_pre": gain(ks[9], D),
        "mix_g_post": gain(ks[10], D),
        "w_in": nrm(ks[11], (DEPTH, D, 3 * D_MIX), D ** -0.5),
        "w_out": nrm(ks[12], (DEPTH, D_MIX, D), D_MIX ** -0.5),
        "lam_q1": nrm(ks[13], (DEPTH, HEAD_DIM), 0.1),
        "lam_k1": nrm(ks[14], (DEPTH, HEAD_DIM), 0.1),
        "lam_q2": nrm(ks[15], (DEPTH, HEAD_DIM), 0.1),
        "lam_k2": nrm(ks[16], (DEPTH, HEAD_DIM), 0.1),
        "diff_subln": gain(ks[17], DIFF_V_DIM),
        "sb_beta": gain(ks[18], D_SB),
        "ffn2_g_pre": gain(ks[19], D),
        "ffn2_g_post": gain(ks[20], D),
        "ffn2_w_gate": nrm(ks[21], (DEPTH, D, D_FF), D ** -0.5),
        "ffn2_w_up": nrm(ks[22], (DEPTH, D, D_FF), D ** -0.5),
        "ffn2_w_down": nrm(ks[23], (DEPTH, D_FF, D), D_FF ** -0.5),
    }


def reference(x, c, w_ada, b_ada, ffn1_g_pre, ffn1_g_post, ffn1_w_gate, ffn1_w_up, ffn1_w_down,
              mix_g_pre, mix_g_post, w_in, w_out, lam_q1, lam_k1, lam_q2, lam_k2, diff_subln, sb_beta,
              ffn2_g_pre, ffn2_g_post, ffn2_w_gate, ffn2_w_up, ffn2_w_down):
    for l in range(DEPTH):
        lambda_init = 0.8 - 0.6 * float(np.exp(-0.3 * l))
        mod = jax.nn.silu(c) @ w_ada[l] + b_ada[l]
        (sh1, sc1, g1, sh2, sc2, g2, sh3, sc3, g3) = jnp.split(mod, N_MOD, axis=-1)
        x = sandwich(x, lambda h: swiglu(h, ffn1_w_gate[l], ffn1_w_up[l], ffn1_w_down[l]),
                     ffn1_g_pre[l], ffn1_g_post[l], sh1, sc1, g1, FFN_RES_WEIGHT)
        x = sandwich(x, lambda h: hybrid_mixer(h, w_in[l], w_out[l], lam_q1[l], lam_k1[l], lam_q2[l],
                                               lam_k2[l], diff_subln[l], sb_beta[l], lambda_init),
                     mix_g_pre[l], mix_g_post[l], sh2, sc2, g2, MIX_RES_WEIGHT)
        x = sandwich(x, lambda h: swiglu(h, ffn2_w_gate[l], ffn2_w_up[l], ffn2_w_down[l]),
                     ffn2_g_pre[l], ffn2_g_post[l], sh3, sc3, g3, FFN_RES_WEIGHT)
    return x
```

```python
import functools
import math

import jax
import jax.numpy as jnp
from jax import lax
from jax.experimental import pallas as pl
from jax.experimental.pallas import tpu as pltpu

F32 = jnp.float32
BF16 = jnp.bfloat16

HEAD_DIM = 64
DIFF_HEADS = 4
SB_HEADS = 8
N_MOD = 9
RMS_EPS = 1e-6
FFN_RES_WEIGHT = 0.5
MIX_RES_WEIGHT = 1.0

LANES = 128
VMEM_LIMIT_BYTES = 56 * 1024 * 1024

ADA_TN = 1024
FFN_TM = 512
FFN_TF = 256
ATT_TQ = 256

SB_DEAD_LOG = -110.0


def _cparams(semantics):
    return pltpu.CompilerParams(dimension_semantics=semantics, vmem_limit_bytes=VMEM_LIMIT_BYTES)


def _resident(block_shape, index_map):
    return pl.BlockSpec(block_shape, index_map, pipeline_mode=pl.Buffered(1))


def _sigmoid(x):
    return 1.0 / (1.0 + jnp.exp(-x))


def _rms(x, gain):
    ms = jnp.mean(x * x, axis=-1, keepdims=True)
    return x * lax.rsqrt(ms + RMS_EPS) * gain


def _adaln_kernel(c_ref, w_ref, b_ref, lamp_ref, mod_ref, lam_ref, *, lambda_init):
    c = c_ref[...]
    s = c * _sigmoid(c)
    mod_ref[...] = jnp.dot(s, w_ref[...], preferred_element_type=F32,
                           precision=lax.Precision.HIGHEST) + b_ref[...]
    lp = lamp_ref[...]
    d1 = jnp.sum(lp[0:1] * lp[1:2], axis=-1, keepdims=True)
    d2 = jnp.sum(lp[2:3] * lp[3:4], axis=-1, keepdims=True)
    lam = jnp.exp(d1) - jnp.exp(d2) + lambda_init
    lam_ref[...] = jnp.broadcast_to(lam, lam_ref.shape)


def _adaln(c, w_ada, b_ada, lam_params, lambda_init):
    bsz, d = c.shape
    n = w_ada.shape[1]
    tn = ADA_TN
    return pl.pallas_call(
        functools.partial(_adaln_kernel, lambda_init=lambda_init),
        grid=(n // tn,),
        in_specs=[
            pl.BlockSpec((bsz, d), lambda j: (0, 0)),
            pl.BlockSpec((d, tn), lambda j: (0, j)),
            pl.BlockSpec((1, tn), lambda j: (0, j)),
            pl.BlockSpec(lam_params.shape, lambda j: (0, 0)),
        ],
        out_specs=[
            pl.BlockSpec((bsz, tn), lambda j: (0, j)),
            pl.BlockSpec((8, LANES), lambda j: (0, 0)),
        ],
        out_shape=[
            jax.ShapeDtypeStruct((bsz, n), F32),
            jax.ShapeDtypeStruct((8, LANES), F32),
        ],
        compiler_params=_cparams(("arbitrary",)),
        name="adaln",
    )(c, w_ada, b_ada.reshape(1, n), lam_params)


def _modulated_norm(x, g_pre, scale, shift):
    return _rms(x, g_pre) * (1.0 + scale) + shift


def _ffn_kernel(x_ref, sh_ref, sc_ref, gt_ref, gpre_ref, gpost_ref, wg_ref, wu_ref, wd_ref,
                o_ref, h_scr, a_scr, *, tf):
    x = x_ref[...]
    h_scr[...] = _modulated_norm(x, gpre_ref[...], sc_ref[...], sh_ref[...]).astype(BF16)
    d_ff = wg_ref.shape[1]
    for j in range(d_ff // tf):
        cols = slice(j * tf, (j + 1) * tf)
        h = h_scr[...]
        g = jnp.dot(h, wg_ref[:, cols], preferred_element_type=F32)
        u = jnp.dot(h, wu_ref[:, cols], preferred_element_type=F32)
        a_scr[:, cols] = (g * _sigmoid(g) * u).astype(BF16)
    y = jnp.dot(a_scr[...], wd_ref[...], preferred_element_type=F32)
    o_ref[...] = x + FFN_RES_WEIGHT * gt_ref[...] * _rms(y, gpost_ref[...])


def _ffn(x, mod4, k0, g_pre, g_post, w_gate, w_up, w_down):
    bsz, s, d = x.shape
    d_ff = w_gate.shape[1]
    tm, tf = FFN_TM, FFN_TF
    mod_spec = lambda k: pl.BlockSpec((None, None, 1, d), lambda b, i: (b, k, 0, 0))
    row = lambda: _resident((1, d), lambda b, i: (0, 0))
    return pl.pallas_call(
        functools.partial(_ffn_kernel, tf=tf),
        grid=(bsz, s // tm),
        in_specs=[
            pl.BlockSpec((None, tm, d), lambda b, i: (b, i, 0)),
            mod_spec(k0), mod_spec(k0 + 1), mod_spec(k0 + 2),
            row(), row(),
            _resident((d, d_ff), lambda b, i: (0, 0)),
            _resident((d, d_ff), lambda b, i: (0, 0)),
            _resident((d_ff, d), lambda b, i: (0, 0)),
        ],
        out_specs=pl.BlockSpec((None, tm, d), lambda b, i: (b, i, 0)),
        out_shape=jax.ShapeDtypeStruct(x.shape, F32),
        scratch_shapes=[pltpu.VMEM((tm, d), BF16), pltpu.VMEM((tm, d_ff), BF16)],
        compiler_params=_cparams(("parallel", "parallel")),
        name="ffn",
    )(x, mod4, mod4, mod4, g_pre.reshape(1, d), g_post.reshape(1, d),
      w_gate.astype(BF16), w_up.astype(BF16), w_down.astype(BF16))


def _mix_in_kernel(x_ref, sh_ref, sc_ref, gpre_ref, w_ref, o_ref, h_scr, *, group, q_groups, q_scale):
    h_scr[...] = _modulated_norm(x_ref[...], gpre_ref[...], sc_ref[...], sh_ref[...]).astype(BF16)
    for g in range(w_ref.shape[1] // group):
        cols = slice(g * group, (g + 1) * group)
        p = jnp.dot(h_scr[...], w_ref[:, cols], preferred_element_type=F32)
        if g in q_groups:
            p = p * q_scale
        o_ref[:, cols] = p.astype(BF16)


def _mix_in(x, mod4, k0, g_pre, w_in):
    bsz, s, d = x.shape
    n = w_in.shape[1]
    group = n // 6
    tm = FFN_TM
    mod_spec = lambda k: pl.BlockSpec((None, None, 1, d), lambda b, i: (b, k, 0, 0))
    return pl.pallas_call(
        functools.partial(_mix_in_kernel, group=group, q_groups=(0, 3), q_scale=HEAD_DIM ** -0.5),
        grid=(bsz, s // tm),
        in_specs=[
            pl.BlockSpec((None, tm, d), lambda b, i: (b, i, 0)),
            mod_spec(k0), mod_spec(k0 + 1),
            _resident((1, d), lambda b, i: (0, 0)),
            _resident((d, n), lambda b, i: (0, 0)),
        ],
        out_specs=pl.BlockSpec((None, tm, n), lambda b, i: (b, i, 0)),
        out_shape=jax.ShapeDtypeStruct((bsz, s, n), BF16),
        scratch_shapes=[pltpu.VMEM((tm, d), BF16)],
        compiler_params=_cparams(("parallel", "parallel")),
        name="mix_in",
    )(x, mod4, mod4, g_pre.reshape(1, d), w_in.astype(BF16))


def _dot_nt(a, b):
    return lax.dot_general(a, b, (((1,), (1,)), ((), ())), preferred_element_type=F32)


def _split_heads(x):
    lane = lax.broadcasted_iota(jnp.int32, x.shape, 1)
    zero = jnp.zeros_like(x)
    return jnp.where(lane < HEAD_DIM, x, zero), jnp.where(lane >= HEAD_DIM, x, zero)


def _diff_attn_kernel(q_ref, k_ref, v_ref, lam_ref, subln_ref, o_ref, *, t, slopes, out_scale):
    head = pl.program_id(1)
    i = pl.program_id(2)
    slope = jnp.float32(slopes[-1])
    for hh in range(len(slopes) - 2, -1, -1):
        slope = jnp.where(head == hh, jnp.float32(slopes[hh]), slope)

    q0, q1 = _split_heads(q_ref[...])
    r = lax.broadcasted_iota(jnp.int32, (t, t), 0)
    c = lax.broadcasted_iota(jnp.int32, (t, t), 1)
    rel = (r - c).astype(F32) * slope
    causal = c <= r

    def chunk(j, carry, masked):
        start = pl.multiple_of(j * t, t)
        k = k_ref[pl.ds(start, t), :]
        v = v_ref[pl.ds(start, t), :]
        off = slope * ((i - j) * t).astype(F32)

        def one(qm, m, l, a):
            s = _dot_nt(qm, k) - rel
            if masked:
                s = jnp.where(causal, s, -jnp.inf)
            m_new = jnp.maximum(m, jnp.max(s, axis=-1, keepdims=True) - off)
            p = jnp.exp(s - (m_new + off))
            alpha = jnp.exp(m - m_new)
            l = alpha * l + jnp.sum(p, axis=-1, keepdims=True)
            a = alpha * a + jnp.dot(p.astype(BF16), v, preferred_element_type=F32)
            return m_new, l, a

        m0, l0, a0, m1, l1, a1 = carry
        return one(q0, m0, l0, a0) + one(q1, m1, l1, a1)

    col = lambda val: jnp.full((t, 1), val, F32)
    acc = lambda: jnp.zeros((t, LANES), F32)
    init = (col(-jnp.inf), col(0.0), acc(), col(-jnp.inf), col(0.0), acc())
    carry = lax.fori_loop(0, i, lambda j, cr: chunk(j, cr, False), init)
    m0, l0, a0, m1, l1, a1 = chunk(i, carry, True)

    lam = lam_ref[0:1, :]
    o = a0 / l0 - lam * (a1 / l1)
    o_ref[...] = (_rms(o, subln_ref[...]) * out_scale).astype(o_ref.dtype)


def _diff_attn(proj, lam, subln, lambda_init):
    bsz, s, _ = proj.shape
    t = ATT_TQ
    nh = DIFF_HEADS
    slopes = tuple(2.0 ** (-8.0 * (h + 1) / nh) for h in range(nh))
    return pl.pallas_call(
        functools.partial(_diff_attn_kernel, t=t, slopes=slopes, out_scale=1.0 - lambda_init),
        grid=(bsz, nh, s // t),
        in_specs=[
            pl.BlockSpec((None, t, LANES), lambda b, h, i: (b, i, h)),
            pl.BlockSpec((None, s, LANES), lambda b, h, i: (b, 0, nh + h)),
            pl.BlockSpec((None, s, LANES), lambda b, h, i: (b, 0, 2 * nh + h)),
            pl.BlockSpec((8, LANES), lambda b, h, i: (0, 0)),
            pl.BlockSpec((1, LANES), lambda b, h, i: (0, 0)),
        ],
        out_specs=pl.BlockSpec((None, t, LANES), lambda b, h, i: (b, i, h)),
        out_shape=jax.ShapeDtypeStruct((bsz, s, nh * LANES), BF16),
        compiler_params=_cparams(("parallel", "parallel", "arbitrary")),
        name="diff_attn",
    )(proj, proj, proj, lam, subln.reshape(1, LANES))


def _sb_attn_kernel(q_ref, k_ref, v_ref, o_ref, *, t):
    i = pl.program_id(2)
    qa, qb = _split_heads(q_ref[...])
    r = lax.broadcasted_iota(jnp.int32, (t, t), 0)
    c = lax.broadcasted_iota(jnp.int32, (t, t), 1)
    strict = c < r
    after = (r > c).astype(BF16)

    def chunk(j, ca, cb, acc, masked):
        start = pl.multiple_of(j * t, t)
        k = k_ref[pl.ds(start, t), :]
        va, vb = _split_heads(v_ref[pl.ds(start, t), :])

        def one(qm, carry, vm):
            z = _dot_nt(qm, k)
            soft = jnp.log(1.0 + jnp.exp(-jnp.abs(z)))
            log_beta = jnp.minimum(z, 0.0) - soft
            x = log_beta - z
            if masked:
                x = jnp.where(strict, x, 0.0)
            xh = x.astype(BF16)
            xl = (x - xh.astype(F32)).astype(BF16)
            inner = (jnp.dot(xh, after, preferred_element_type=F32)
                     + jnp.dot(xl, after, preferred_element_type=F32))
            att = jnp.exp(log_beta + (inner + carry))
            if masked:
                att = jnp.where(strict, att, 0.0)
            carry = carry + jnp.sum(x, axis=-1, keepdims=True)
            return carry, jnp.dot(att.astype(BF16), vm, preferred_element_type=F32)

        ca, pa = one(qa, ca, va)
        cb, pb = one(qb, cb, vb)
        return ca, cb, acc + pa + pb

    zero_col = jnp.zeros((t, 1), F32)
    ca, cb, acc = chunk(i, zero_col, zero_col, jnp.zeros((t, LANES), F32), True)

    def alive(ca, cb):
        return jnp.max(jnp.maximum(ca, cb)) > SB_DEAD_LOG

    def cond(state):
        j, _, _, _, live = state
        return jnp.logical_and(j >= 0, live)

    def body(state):
        j, ca, cb, acc, _ = state
        ca, cb, acc = chunk(j, ca, cb, acc, False)
        return j - 1, ca, cb, acc, alive(ca, cb)

    _, _, _, acc, _ = lax.while_loop(cond, body, (i - 1, ca, cb, acc, alive(ca, cb)))
    o_ref[...] = acc


def _sb_attn(proj):
    bsz, s, _ = proj.shape
    t = ATT_TQ
    pairs = SB_HEADS * HEAD_DIM // LANES
    base = 3 * DIFF_HEADS
    return pl.pallas_call(
        functools.partial(_sb_attn_kernel, t=t),
        grid=(bsz, pairs, s // t),
        in_specs=[
            pl.BlockSpec((None, t, LANES), lambda b, p, i: (b, i, base + p)),
            pl.BlockSpec((None, s, LANES), lambda b, p, i: (b, 0, base + pairs + p)),
            pl.BlockSpec((None, s, LANES), lambda b, p, i: (b, 0, base + 2 * pairs + p)),
        ],
        out_specs=pl.BlockSpec((None, t, LANES), lambda b, p, i: (b, i, p)),
        out_shape=jax.ShapeDtypeStruct((bsz, s, pairs * LANES), F32),
        compiler_params=_cparams(("parallel", "parallel", "arbitrary")),
        name="sb_attn",
    )(proj, proj, proj)


def _mix_out_kernel(x_ref, od_ref, os_ref, gt_ref, beta_ref, gpost_ref, wd_ref, ws_ref, o_ref):
    osn = _rms(os_ref[...], beta_ref[...]).astype(BF16)
    y = (jnp.dot(od_ref[...], wd_ref[...], preferred_element_type=F32)
         + jnp.dot(osn, ws_ref[...], preferred_element_type=F32))
    o_ref[...] = x_ref[...] + MIX_RES_WEIGHT * gt_ref[...] * _rms(y, gpost_ref[...])


def _mix_out(x, o_d, o_s, mod4, k_gate, sb_beta, g_post, w_out):
    bsz, s, d = x.shape
    dd, ds = o_d.shape[-1], o_s.shape[-1]
    tm = FFN_TM
    w_bf = w_out.astype(BF16)
    return pl.pallas_call(
        _mix_out_kernel,
        grid=(bsz, s // tm),
        in_specs=[
            pl.BlockSpec((None, tm, d), lambda b, i: (b, i, 0)),
            pl.BlockSpec((None, tm, dd), lambda b, i: (b, i, 0)),
            pl.BlockSpec((None, tm, ds), lambda b, i: (b, i, 0)),
            pl.BlockSpec((None, None, 1, d), lambda b, i: (b, k_gate, 0, 0)),
            _resident((1, ds), lambda b, i: (0, 0)),
            _resident((1, d), lambda b, i: (0, 0)),
            _resident((dd, d), lambda b, i: (0, 0)),
            _resident((ds, d), lambda b, i: (0, 0)),
        ],
        out_specs=pl.BlockSpec((None, tm, d), lambda b, i: (b, i, 0)),
        out_shape=jax.ShapeDtypeStruct(x.shape, F32),
        compiler_params=_cparams(("parallel", "parallel")),
        name="mix_out",
    )(x, o_d, o_s, mod4, sb_beta.reshape(1, ds), g_post.reshape(1, d), w_bf[:dd], w_bf[dd:])


def kernel(x, c, w_ada, b_ada, ffn1_g_pre, ffn1_g_post, ffn1_w_gate, ffn1_w_up, ffn1_w_down, mix_g_pre, mix_g_post, w_in, w_out, lam_q1, lam_k1, lam_q2, lam_k2, diff_subln, sb_beta, ffn2_g_pre, ffn2_g_post, ffn2_w_gate, ffn2_w_up, ffn2_w_down):
    bsz, _, d = x.shape
    depth = w_ada.shape[0]
    for l in range(depth):
        lambda_init = 0.8 - 0.6 * math.exp(-0.3 * l)
        lam_params = jnp.stack([lam_q1[l], lam_k1[l], lam_q2[l], lam_k2[l]])
        mod, lam = _adaln(c, w_ada[l], b_ada[l], lam_params, lambda_init)
        mod4 = mod.reshape(bsz, N_MOD, 1, d)
        x = _ffn(x, mod4, 0, ffn1_g_pre[l], ffn1_g_post[l], ffn1_w_gate[l], ffn1_w_up[l], ffn1_w_down[l])
        proj = _mix_in(x, mod4, 3, mix_g_pre[l], w_in[l])
        o_d = _diff_attn(proj, lam, diff_subln[l], lambda_init)
        o_s = _sb_attn(proj)
        x = _mix_out(x, o_d, o_s, mod4, 5, sb_beta[l], mix_g_post[l], w_out[l])
        x = _ffn(x, mod4, 6, ffn2_g_pre[l], ffn2_g_post[l], ffn2_w_gate[l], ffn2_w_up[l], ffn2_w_down[l])
    return x
```

```python
import functools
import math

import jax
import jax.numpy as jnp
from jax import lax
from jax.experimental import pallas as pl
from jax.experimental.pallas import tpu as pltpu

F32 = jnp.float32
BF16 = jnp.bfloat16

HEAD_DIM = 64
DIFF_HEADS = 4
SB_HEADS = 8
N_MOD = 9
RMS_EPS = 1e-6
FFN_RES_WEIGHT = 0.5
MIX_RES_WEIGHT = 1.0

LANES = 128
VMEM_LIMIT_BYTES = 56 * 1024 * 1024

ADA_TN = 1024
FFN_TM = 512
FFN_TF = 256
DIFF_T = 512
SB_T = 256

LOG2E = math.log2(math.e)
DIFF_Q_SCALE = LOG2E * HEAD_DIM ** -0.5
SB_Q_SCALE = HEAD_DIM ** -0.5

SB_DEAD_LOG = -110.0


def _cparams(semantics):
    return pltpu.CompilerParams(dimension_semantics=semantics, vmem_limit_bytes=VMEM_LIMIT_BYTES)


def _resident(block_shape, index_map):
    return pl.BlockSpec(block_shape, index_map, pipeline_mode=pl.Buffered(1))


def _sigmoid(x):
    return 1.0 / (1.0 + jnp.exp(-x))


def _rms(x, gain):
    ms = jnp.mean(x * x, axis=-1, keepdims=True)
    return x * lax.rsqrt(ms + RMS_EPS) * gain


def _adaln_kernel(c_ref, w_ref, b_ref, lamp_ref, mod_ref, lam_ref, *, lambda_init):
    c = c_ref[...]
    s = c * _sigmoid(c)
    mod_ref[...] = jnp.dot(s, w_ref[...], preferred_element_type=F32,
                           precision=lax.Precision.HIGHEST) + b_ref[...]
    lp = lamp_ref[...]
    d1 = jnp.sum(lp[0:1] * lp[1:2], axis=-1, keepdims=True)
    d2 = jnp.sum(lp[2:3] * lp[3:4], axis=-1, keepdims=True)
    lam = jnp.exp(d1) - jnp.exp(d2) + lambda_init
    lam_ref[...] = jnp.broadcast_to(lam, lam_ref.shape)


def _adaln(c, w_ada, b_ada, lam_params, lambda_init):
    bsz, d = c.shape
    n = w_ada.shape[1]
    tn = ADA_TN
    return pl.pallas_call(
        functools.partial(_adaln_kernel, lambda_init=lambda_init),
        grid=(n // tn,),
        in_specs=[
            pl.BlockSpec((bsz, d), lambda j: (0, 0)),
            pl.BlockSpec((d, tn), lambda j: (0, j)),
            pl.BlockSpec((1, tn), lambda j: (0, j)),
            pl.BlockSpec(lam_params.shape, lambda j: (0, 0)),
        ],
        out_specs=[
            pl.BlockSpec((bsz, tn), lambda j: (0, j)),
            pl.BlockSpec((8, LANES), lambda j: (0, 0)),
        ],
        out_shape=[
            jax.ShapeDtypeStruct((bsz, n), F32),
            jax.ShapeDtypeStruct((8, LANES), F32),
        ],
        compiler_params=_cparams(("arbitrary",)),
        name="adaln",
    )(c, w_ada, b_ada.reshape(1, n), lam_params)


def _modulated_norm(x, g_pre, scale, shift):
    return _rms(x, g_pre) * (1.0 + scale) + shift


def _ffn_kernel(x_ref, sh_ref, sc_ref, gt_ref, gpre_ref, gpost_ref, wg_ref, wu_ref, wd_ref,
                o_ref, h_scr, a_scr, *, tf):
    x = x_ref[...]
    h_scr[...] = _modulated_norm(x, gpre_ref[...], sc_ref[...], sh_ref[...]).astype(BF16)
    d_ff = wg_ref.shape[1]
    for j in range(d_ff // tf):
        cols = slice(j * tf, (j + 1) * tf)
        h = h_scr[...]
        g = jnp.dot(h, wg_ref[:, cols], preferred_element_type=F32)
        u = jnp.dot(h, wu_ref[:, cols], preferred_element_type=F32)
        a_scr[:, cols] = (g * _sigmoid(g) * u).astype(BF16)
    y = jnp.dot(a_scr[...], wd_ref[...], preferred_element_type=F32)
    o_ref[...] = x + FFN_RES_WEIGHT * gt_ref[...] * _rms(y, gpost_ref[...])


def _ffn(x, mod4, k0, g_pre, g_post, w_gate, w_up, w_down):
    bsz, s, d = x.shape
    d_ff = w_gate.shape[1]
    tm, tf = FFN_TM, FFN_TF
    mod_spec = lambda k: pl.BlockSpec((None, None, 1, d), lambda b, i: (b, k, 0, 0))
    row = lambda: _resident((1, d), lambda b, i: (0, 0))
    return pl.pallas_call(
        functools.partial(_ffn_kernel, tf=tf),
        grid=(bsz, s // tm),
        in_specs=[
            pl.BlockSpec((None, tm, d), lambda b, i: (b, i, 0)),
            mod_spec(k0), mod_spec(k0 + 1), mod_spec(k0 + 2),
            row(), row(),
            _resident((d, d_ff), lambda b, i: (0, 0)),
            _resident((d, d_ff), lambda b, i: (0, 0)),
            _resident((d_ff, d), lambda b, i: (0, 0)),
        ],
        out_specs=pl.BlockSpec((None, tm, d), lambda b, i: (b, i, 0)),
        out_shape=jax.ShapeDtypeStruct(x.shape, F32),
        scratch_shapes=[pltpu.VMEM((tm, d), BF16), pltpu.VMEM((tm, d_ff), BF16)],
        compiler_params=_cparams(("parallel", "parallel")),
        name="ffn",
    )(x, mod4, mod4, mod4, g_pre.reshape(1, d), g_post.reshape(1, d),
      w_gate.astype(BF16), w_up.astype(BF16), w_down.astype(BF16))


def _mix_in_kernel(x_ref, sh_ref, sc_ref, gpre_ref, w_ref, o_ref, h_scr, *, group, group_scale):
    h_scr[...] = _modulated_norm(x_ref[...], gpre_ref[...], sc_ref[...], sh_ref[...]).astype(BF16)
    for g in range(w_ref.shape[1] // group):
        cols = slice(g * group, (g + 1) * group)
        p = jnp.dot(h_scr[...], w_ref[:, cols], preferred_element_type=F32)
        if g in group_scale:
            p = p * group_scale[g]
        o_ref[:, cols] = p.astype(BF16)


def _mix_in(x, mod4, k0, g_pre, w_in):
    bsz, s, d = x.shape
    n = w_in.shape[1]
    group = n // 6
    tm = FFN_TM
    mod_spec = lambda k: pl.BlockSpec((None, None, 1, d), lambda b, i: (b, k, 0, 0))
    return pl.pallas_call(
        functools.partial(_mix_in_kernel, group=group,
                          group_scale={0: DIFF_Q_SCALE, 3: SB_Q_SCALE}),
        grid=(bsz, s // tm),
        in_specs=[
            pl.BlockSpec((None, tm, d), lambda b, i: (b, i, 0)),
            mod_spec(k0), mod_spec(k0 + 1),
            _resident((1, d), lambda b, i: (0, 0)),
            _resident((d, n), lambda b, i: (0, 0)),
        ],
        out_specs=pl.BlockSpec((None, tm, n), lambda b, i: (b, i, 0)),
        out_shape=jax.ShapeDtypeStruct((bsz, s, n), BF16),
        scratch_shapes=[pltpu.VMEM((tm, d), BF16)],
        compiler_params=_cparams(("parallel", "parallel")),
        name="mix_in",
    )(x, mod4, mod4, g_pre.reshape(1, d), w_in.astype(BF16))


def _dot_nt(a, b):
    return lax.dot_general(a, b, (((1,), (1,)), ((), ())), preferred_element_type=F32)


def _split_heads(x):
    lane = lax.broadcasted_iota(jnp.int32, x.shape, 1)
    zero = jnp.zeros_like(x)
    return jnp.where(lane < HEAD_DIM, x, zero), jnp.where(lane >= HEAD_DIM, x, zero)


def _diff_attn_kernel(q_ref, k_ref, v_ref, lam_ref, subln_ref, o_ref, *, t, slopes, out_scale):
    head = pl.program_id(1)
    i = pl.program_id(2)
    slope = jnp.float32(slopes[-1])
    for hh in range(len(slopes) - 2, -1, -1):
        slope = jnp.where(head == hh, jnp.float32(slopes[hh]), slope)
    slope = slope * LOG2E

    q0, q1 = _split_heads(q_ref[...])
    r = lax.broadcasted_iota(jnp.int32, (t, t), 0)
    c = lax.broadcasted_iota(jnp.int32, (t, t), 1)
    rel = (r - c).astype(F32) * slope
    causal = c <= r

    def lane_partial_sum(p):
        parts = [p[:, n * LANES:(n + 1) * LANES] for n in range(t // LANES)]
        return functools.reduce(lambda a, b: a + b, parts)

    def chunk(j, carry, masked):
        start = pl.multiple_of(j * t, t)
        k = k_ref[pl.ds(start, t), :]
        v = v_ref[pl.ds(start, t), :]
        off = slope * ((i - j) * t).astype(F32)

        def one(qm, m, l, a):
            s = _dot_nt(qm, k) - rel
            if masked:
                s = jnp.where(causal, s, -jnp.inf)
            m_new = jnp.maximum(m, jnp.max(s, axis=-1, keepdims=True) - off)
            p = jnp.exp2(s - (m_new + off))
            alpha = jnp.exp2(m - m_new)
            l = alpha * l + lane_partial_sum(p)
            a = alpha * a + jnp.dot(p.astype(BF16), v, preferred_element_type=F32)
            return m_new, l, a

        m0, l0, a0, m1, l1, a1 = carry
        return one(q0, m0, l0, a0) + one(q1, m1, l1, a1)

    col = lambda val: jnp.full((t, 1), val, F32)
    acc = lambda: jnp.zeros((t, LANES), F32)
    init = (col(-jnp.inf), acc(), acc(), col(-jnp.inf), acc(), acc())
    carry = lax.fori_loop(0, i, lambda j, cr: chunk(j, cr, False), init)
    m0, l0, a0, m1, l1, a1 = chunk(i, carry, True)

    lam = lam_ref[0:1, :]
    l0 = jnp.sum(l0, axis=-1, keepdims=True)
    l1 = jnp.sum(l1, axis=-1, keepdims=True)
    o = a0 / l0 - lam * (a1 / l1)
    o_ref[...] = (_rms(o, subln_ref[...]) * out_scale).astype(o_ref.dtype)


def _diff_attn(proj, lam, subln, lambda_init):
    bsz, s, _ = proj.shape
    t = DIFF_T
    nh = DIFF_HEADS
    slopes = tuple(2.0 ** (-8.0 * (h + 1) / nh) for h in range(nh))
    return pl.pallas_call(
        functools.partial(_diff_attn_kernel, t=t, slopes=slopes, out_scale=1.0 - lambda_init),
        grid=(bsz, nh, s // t),
        in_specs=[
            pl.BlockSpec((None, t, LANES), lambda b, h, i: (b, i, h)),
            pl.BlockSpec((None, s, LANES), lambda b, h, i: (b, 0, nh + h)),
            pl.BlockSpec((None, s, LANES), lambda b, h, i: (b, 0, 2 * nh + h)),
            pl.BlockSpec((8, LANES), lambda b, h, i: (0, 0)),
            pl.BlockSpec((1, LANES), lambda b, h, i: (0, 0)),
        ],
        out_specs=pl.BlockSpec((None, t, LANES), lambda b, h, i: (b, i, h)),
        out_shape=jax.ShapeDtypeStruct((bsz, s, nh * LANES), BF16),
        compiler_params=_cparams(("parallel", "parallel", "arbitrary")),
        name="diff_attn",
    )(proj, proj, proj, lam, subln.reshape(1, LANES))


def _sb_attn_kernel(q_ref, k_ref, v_ref, o_ref, *, t):
    i = pl.program_id(2)
    qa, qb = _split_heads(q_ref[...])
    r = lax.broadcasted_iota(jnp.int32, (t, t), 0)
    c = lax.broadcasted_iota(jnp.int32, (t, t), 1)
    strict = c < r
    after = (r > c).astype(BF16)

    def chunk(j, ca, cb, acc, masked):
        start = pl.multiple_of(j * t, t)
        k = k_ref[pl.ds(start, t), :]
        va, vb = _split_heads(v_ref[pl.ds(start, t), :])

        def one(qm, carry, vm):
            z = _dot_nt(qm, k)
            soft = jnp.log(1.0 + jnp.exp(-jnp.abs(z)))
            log_beta = jnp.minimum(z, 0.0) - soft
            x = log_beta - z
            if masked:
                x = jnp.where(strict, x, 0.0)
            xh = x.astype(BF16)
            xl = (x - xh.astype(F32)).astype(BF16)
            inner = (jnp.dot(xh, after, preferred_element_type=F32)
                     + jnp.dot(xl, after, preferred_element_type=F32))
            att = jnp.exp(log_beta + (inner + carry))
            if masked:
                att = jnp.where(strict, att, 0.0)
            carry = carry + jnp.sum(x, axis=-1, keepdims=True)
            return carry, jnp.dot(att.astype(BF16), vm, preferred_element_type=F32)

        ca, pa = one(qa, ca, va)
        cb, pb = one(qb, cb, vb)
        return ca, cb, acc + pa + pb

    zero_col = jnp.zeros((t, 1), F32)
    ca, cb, acc = chunk(i, zero_col, zero_col, jnp.zeros((t, LANES), F32), True)

    def alive(ca, cb):
        return jnp.max(jnp.maximum(ca, cb)) > SB_DEAD_LOG

    def cond(state):
        j, _, _, _, live = state
        return jnp.logical_and(j >= 0, live)

    def body(state):
        j, ca, cb, acc, _ = state
        ca, cb, acc = chunk(j, ca, cb, acc, False)
        return j - 1, ca, cb, acc, alive(ca, cb)

    _, _, _, acc, _ = lax.while_loop(cond, body, (i - 1, ca, cb, acc, alive(ca, cb)))
    o_ref[...] = acc


def _sb_attn(proj):
    bsz, s, _ = proj.shape
    t = SB_T
    pairs = SB_HEADS * HEAD_DIM // LANES
    base = 3 * DIFF_HEADS
    return pl.pallas_call(
        functools.partial(_sb_attn_kernel, t=t),
        grid=(bsz, pairs, s // t),
        in_specs=[
            pl.BlockSpec((None, t, LANES), lambda b, p, i: (b, i, base + p)),
            pl.BlockSpec((None, s, LANES), lambda b, p, i: (b, 0, base + pairs + p)),
            pl.BlockSpec((None, s, LANES), lambda b, p, i: (b, 0, base + 2 * pairs + p)),
        ],
        out_specs=pl.BlockSpec((None, t, LANES), lambda b, p, i: (b, i, p)),
        out_shape=jax.ShapeDtypeStruct((bsz, s, pairs * LANES), F32),
        compiler_params=_cparams(("parallel", "parallel", "arbitrary")),
        name="sb_attn",
    )(proj, proj, proj)


def _mix_out_kernel(x_ref, od_ref, os_ref, gt_ref, beta_ref, gpost_ref, wd_ref, ws_ref, o_ref):
    osn = _rms(os_ref[...], beta_ref[...]).astype(BF16)
    y = (jnp.dot(od_ref[...], wd_ref[...], preferred_element_type=F32)
         + jnp.dot(osn, ws_ref[...], preferred_element_type=F32))
    o_ref[...] = x_ref[...] + MIX_RES_WEIGHT * gt_ref[...] * _rms(y, gpost_ref[...])


def _mix_out(x, o_d, o_s, mod4, k_gate, sb_beta, g_post, w_out):
    bsz, s, d = x.shape
    dd, ds = o_d.shape[-1], o_s.shape[-1]
    tm = FFN_TM
    w_bf = w_out.astype(BF16)
    return pl.pallas_call(
        _mix_out_kernel,
        grid=(bsz, s // tm),
        in_specs=[
            pl.BlockSpec((None, tm, d), lambda b, i: (b, i, 0)),
            pl.BlockSpec((None, tm, dd), lambda b, i: (b, i, 0)),
            pl.BlockSpec((None, tm, ds), lambda b, i: (b, i, 0)),
            pl.BlockSpec((None, None, 1, d), lambda b, i: (b, k_gate, 0, 0)),
            _resident((1, ds), lambda b, i: (0, 0)),
            _resident((1, d), lambda b, i: (0, 0)),
            _resident((dd, d), lambda b, i: (0, 0)),
            _resident((ds, d), lambda b, i: (0, 0)),
        ],
        out_specs=pl.BlockSpec((None, tm, d), lambda b, i: (b, i, 0)),
        out_shape=jax.ShapeDtypeStruct(x.shape, F32),
        compiler_params=_cparams(("parallel", "parallel")),
        name="mix_out",
    )(x, o_d, o_s, mod4, sb_beta.reshape(1, ds), g_post.reshape(1, d), w_bf[:dd], w_bf[dd:])


def kernel(x, c, w_ada, b_ada, ffn1_g_pre, ffn1_g_post, ffn1_w_gate, ffn1_w_up, ffn1_w_down, mix_g_pre, mix_g_post, w_in, w_out, lam_q1, lam_k1, lam_q2, lam_k2, diff_subln, sb_beta, ffn2_g_pre, ffn2_g_post, ffn2_w_gate, ffn2_w_up, ffn2_w_down):
    bsz, _, d = x.shape
    depth = w_ada.shape[0]
    for l in range(depth):
        lambda_init = 0.8 - 0.6 * math.exp(-0.3 * l)
        lam_params = jnp.stack([lam_q1[l], lam_k1[l], lam_q2[l], lam_k2[l]])
        mod, lam = _adaln(c, w_ada[l], b_ada[l], lam_params, lambda_init)
        mod4 = mod.reshape(bsz, N_MOD, 1, d)
        x = _ffn(x, mod4, 0, ffn1_g_pre[l], ffn1_g_post[l], ffn1_w_gate[l], ffn1_w_up[l], ffn1_w_down[l])
        proj = _mix_in(x, mod4, 3, mix_g_pre[l], w_in[l])
        o_d = _diff_attn(proj, lam, diff_subln[l], lambda_init)
        o_s = _sb_attn(proj)
        x = _mix_out(x, o_d, o_s, mod4, 5, sb_beta[l], mix_g_post[l], w_out[l])
        x = _ffn(x, mod4, 6, ffn2_g_pre[l], ffn2_g_post[l], ffn2_w_gate[l], ffn2_w_up[l], ffn2_w_down[l])
    return x
```

```python
import functools
import math

import jax
import jax.numpy as jnp
from jax import lax
from jax.experimental import pallas as pl
from jax.experimental.pallas import tpu as pltpu

F32 = jnp.float32
BF16 = jnp.bfloat16

HEAD_DIM = 64
DIFF_HEADS = 4
SB_HEADS = 8
N_MOD = 9
RMS_EPS = 1e-6
FFN_RES_WEIGHT = 0.5
MIX_RES_WEIGHT = 1.0

LANES = 128
VMEM_LIMIT_BYTES = 56 * 1024 * 1024

ADA_TN = 1024
FFN_TM = 512
FFN_TF = 256
DIFF_T = 512
SB_T = 256

LOG2E = math.log2(math.e)
DIFF_Q_SCALE = LOG2E * HEAD_DIM ** -0.5
SB_Q_SCALE = HEAD_DIM ** -0.5

SB_DEAD_LOG = -110.0
DIFF_DEAD_LOG2 = -150.0
NORM_BOUND_MARGIN = 1.01
NORM_BOUND_SLACK = 1.0


def _cparams(semantics):
    return pltpu.CompilerParams(dimension_semantics=semantics, vmem_limit_bytes=VMEM_LIMIT_BYTES)


def _resident(block_shape, index_map):
    return pl.BlockSpec(block_shape, index_map, pipeline_mode=pl.Buffered(1))


def _sigmoid(x):
    return 1.0 / (1.0 + jnp.exp(-x))


def _rms(x, gain):
    ms = jnp.mean(x * x, axis=-1, keepdims=True)
    return x * lax.rsqrt(ms + RMS_EPS) * gain


def _adaln_kernel(c_ref, w_ref, b_ref, lamp_ref, mod_ref, lam_ref, *, lambda_init):
    c = c_ref[...]
    s = c * _sigmoid(c)
    mod_ref[...] = jnp.dot(s, w_ref[...], preferred_element_type=F32,
                           precision=lax.Precision.HIGHEST) + b_ref[...]
    lp = lamp_ref[...]
    d1 = jnp.sum(lp[0:1] * lp[1:2], axis=-1, keepdims=True)
    d2 = jnp.sum(lp[2:3] * lp[3:4], axis=-1, keepdims=True)
    lam = jnp.exp(d1) - jnp.exp(d2) + lambda_init
    lam_ref[...] = jnp.broadcast_to(lam, lam_ref.shape)


def _adaln(c, w_ada, b_ada, lam_params, lambda_init):
    bsz, d = c.shape
    n = w_ada.shape[1]
    tn = ADA_TN
    return pl.pallas_call(
        functools.partial(_adaln_kernel, lambda_init=lambda_init),
        grid=(n // tn,),
        in_specs=[
            pl.BlockSpec((bsz, d), lambda j: (0, 0)),
            pl.BlockSpec((d, tn), lambda j: (0, j)),
            pl.BlockSpec((1, tn), lambda j: (0, j)),
            pl.BlockSpec(lam_params.shape, lambda j: (0, 0)),
        ],
        out_specs=[
            pl.BlockSpec((bsz, tn), lambda j: (0, j)),
            pl.BlockSpec((8, LANES), lambda j: (0, 0)),
        ],
        out_shape=[
            jax.ShapeDtypeStruct((bsz, n), F32),
            jax.ShapeDtypeStruct((8, LANES), F32),
        ],
        compiler_params=_cparams(("arbitrary",)),
        name="adaln",
    )(c, w_ada, b_ada.reshape(1, n), lam_params)


def _modulated_norm(x, g_pre, scale, shift):
    return _rms(x, g_pre) * (1.0 + scale) + shift


def _ffn_kernel(x_ref, sh_ref, sc_ref, gt_ref, gpre_ref, gpost_ref, wg_ref, wu_ref, wd_ref,
                o_ref, h_scr, a_scr, *, tf):
    x = x_ref[...]
    h_scr[...] = _modulated_norm(x, gpre_ref[...], sc_ref[...], sh_ref[...]).astype(BF16)
    d_ff = wg_ref.shape[1]
    for j in range(d_ff // tf):
        cols = slice(j * tf, (j + 1) * tf)
        h = h_scr[...]
        g = jnp.dot(h, wg_ref[:, cols], preferred_element_type=F32)
        u = jnp.dot(h, wu_ref[:, cols], preferred_element_type=F32)
        a_scr[:, cols] = (g * _sigmoid(g) * u).astype(BF16)
    y = jnp.dot(a_scr[...], wd_ref[...], preferred_element_type=F32)
    o_ref[...] = x + FFN_RES_WEIGHT * gt_ref[...] * _rms(y, gpost_ref[...])


def _ffn(x, mod4, k0, g_pre, g_post, w_gate, w_up, w_down):
    bsz, s, d = x.shape
    d_ff = w_gate.shape[1]
    tm, tf = FFN_TM, FFN_TF
    mod_spec = lambda k: pl.BlockSpec((None, None, 1, d), lambda b, i: (b, k, 0, 0))
    row = lambda: _resident((1, d), lambda b, i: (0, 0))
    return pl.pallas_call(
        functools.partial(_ffn_kernel, tf=tf),
        grid=(bsz, s // tm),
        in_specs=[
            pl.BlockSpec((None, tm, d), lambda b, i: (b, i, 0)),
            mod_spec(k0), mod_spec(k0 + 1), mod_spec(k0 + 2),
            row(), row(),
            _resident((d, d_ff), lambda b, i: (0, 0)),
            _resident((d, d_ff), lambda b, i: (0, 0)),
            _resident((d_ff, d), lambda b, i: (0, 0)),
        ],
        out_specs=pl.BlockSpec((None, tm, d), lambda b, i: (b, i, 0)),
        out_shape=jax.ShapeDtypeStruct(x.shape, F32),
        scratch_shapes=[pltpu.VMEM((tm, d), BF16), pltpu.VMEM((tm, d_ff), BF16)],
        compiler_params=_cparams(("parallel", "parallel")),
        name="ffn",
    )(x, mod4, mod4, mod4, g_pre.reshape(1, d), g_post.reshape(1, d),
      w_gate.astype(BF16), w_up.astype(BF16), w_down.astype(BF16))


def _mix_in_kernel(x_ref, sh_ref, sc_ref, gpre_ref, w_ref, o_ref, h_scr, *, group, group_scale):
    h_scr[...] = _modulated_norm(x_ref[...], gpre_ref[...], sc_ref[...], sh_ref[...]).astype(BF16)
    for g in range(w_ref.shape[1] // group):
        cols = slice(g * group, (g + 1) * group)
        p = jnp.dot(h_scr[...], w_ref[:, cols], preferred_element_type=F32)
        if g in group_scale:
            p = p * group_scale[g]
        o_ref[:, cols] = p.astype(BF16)


def _mix_in(x, mod4, k0, g_pre, w_in):
    bsz, s, d = x.shape
    n = w_in.shape[1]
    group = n // 6
    tm = FFN_TM
    mod_spec = lambda k: pl.BlockSpec((None, None, 1, d), lambda b, i: (b, k, 0, 0))
    return pl.pallas_call(
        functools.partial(_mix_in_kernel, group=group,
                          group_scale={0: DIFF_Q_SCALE, 3: SB_Q_SCALE}),
        grid=(bsz, s // tm),
        in_specs=[
            pl.BlockSpec((None, tm, d), lambda b, i: (b, i, 0)),
            mod_spec(k0), mod_spec(k0 + 1),
            _resident((1, d), lambda b, i: (0, 0)),
            _resident((d, n), lambda b, i: (0, 0)),
        ],
        out_specs=pl.BlockSpec((None, tm, n), lambda b, i: (b, i, 0)),
        out_shape=jax.ShapeDtypeStruct((bsz, s, n), BF16),
        scratch_shapes=[pltpu.VMEM((tm, d), BF16)],
        compiler_params=_cparams(("parallel", "parallel")),
        name="mix_in",
    )(x, mod4, mod4, g_pre.reshape(1, d), w_in.astype(BF16))


def _dot_nt(a, b):
    return lax.dot_general(a, b, (((1,), (1,)), ((), ())), preferred_element_type=F32)


def _split_heads(x):
    lane = lax.broadcasted_iota(jnp.int32, x.shape, 1)
    zero = jnp.zeros_like(x)
    return jnp.where(lane < HEAD_DIM, x, zero), jnp.where(lane >= HEAD_DIM, x, zero)


def _diff_attn_kernel(q_ref, k_ref, v_ref, lam_ref, subln_ref, o_ref, kn_scr, *, t, slopes, out_scale):
    head = pl.program_id(1)
    i = pl.program_id(2)
    slope = jnp.float32(slopes[-1])
    for hh in range(len(slopes) - 2, -1, -1):
        slope = jnp.where(head == hh, jnp.float32(slopes[hh]), slope)
    slope = slope * LOG2E

    q0, q1 = _split_heads(q_ref[...])
    r = lax.broadcasted_iota(jnp.int32, (t, t), 0)
    c = lax.broadcasted_iota(jnp.int32, (t, t), 1)
    rel = (r - c).astype(F32) * slope
    causal = c <= r

    def lane_partial_sum(p):
        parts = [p[:, n * LANES:(n + 1) * LANES] for n in range(t // LANES)]
        return functools.reduce(lambda a, b: a + b, parts)

    def chunk(j, carry, masked):
        start = pl.multiple_of(j * t, t)
        k = k_ref[pl.ds(start, t), :]
        v = v_ref[pl.ds(start, t), :]
        off = slope * ((i - j) * t).astype(F32)

        def one(qm, m, l, a):
            s = _dot_nt(qm, k) - rel
            if masked:
                s = jnp.where(causal, s, -jnp.inf)
            m_new = jnp.maximum(m, jnp.max(s, axis=-1, keepdims=True) - off)
            p = jnp.exp2(s - (m_new + off))
            alpha = jnp.exp2(m - m_new)
            l = alpha * l + lane_partial_sum(p)
            a = alpha * a + jnp.dot(p.astype(BF16), v, preferred_element_type=F32)
            return m_new, l, a

        m0, l0, a0, m1, l1, a1 = carry
        return one(q0, m0, l0, a0) + one(q1, m1, l1, a1)

    sel_r = lax.broadcasted_iota(jnp.int32, (LANES, LANES), 0)
    sel_c = lax.broadcasted_iota(jnp.int32, (LANES, LANES), 1)
    sel = jnp.where(sel_c == lax.shift_right_logical(sel_r, HEAD_DIM.bit_length() - 1), 1.0, 0.0).astype(BF16)

    def max_sq_norm(x):
        xf = x.astype(F32)
        n2 = jnp.dot((xf * xf).astype(BF16), sel, preferred_element_type=F32)
        return jnp.max(n2, axis=0, keepdims=True)

    @pl.when(i == 0)
    def _():
        def body(jj, best):
            kk = k_ref[pl.ds(pl.multiple_of(jj * t, t), t), :]
            return jnp.maximum(best, max_sq_norm(kk))
        kn2 = lax.fori_loop(0, k_ref.shape[0] // t, body, jnp.zeros((1, LANES), F32))
        kn_scr[...] = jnp.broadcast_to(kn2, kn_scr.shape)

    bound = jnp.sqrt(max_sq_norm(q_ref[...]) * kn_scr[0:1, :]) * NORM_BOUND_MARGIN + NORM_BOUND_SLACK
    b0, b1 = bound[:, 0:1], bound[:, 1:2]

    col = lambda val: jnp.full((t, 1), val, F32)
    acc = lambda: jnp.zeros((t, LANES), F32)
    init = (col(-jnp.inf), acc(), acc(), col(-jnp.inf), acc(), acc())
    state = chunk(i, init, True)

    def cond(st):
        return jnp.logical_and(st[0] >= 0, st[-1])

    def body(st):
        j, carry = st[0], st[1:-1]
        m0, m1 = carry[0], carry[3]
        slack = jnp.maximum(b0 - jnp.min(m0, axis=0, keepdims=True),
                            b1 - jnp.min(m1, axis=0, keepdims=True))[0, 0]
        next_dist = ((i - j) * t + 1).astype(F32)
        live = slack - slope * next_dist >= DIFF_DEAD_LOG2
        return (j - 1,) + chunk(j, carry, False) + (live,)

    st = lax.while_loop(cond, body, (i - 1,) + state + (True,))
    m0, l0, a0, m1, l1, a1 = st[1:-1]

    lam = lam_ref[0:1, :]
    l0 = jnp.sum(l0, axis=-1, keepdims=True)
    l1 = jnp.sum(l1, axis=-1, keepdims=True)
    o = a0 / l0 - lam * (a1 / l1)
    o_ref[...] = (_rms(o, subln_ref[...]) * out_scale).astype(o_ref.dtype)


def _diff_attn(proj, lam, subln, lambda_init):
    bsz, s, _ = proj.shape
    t = DIFF_T
    nh = DIFF_HEADS
    slopes = tuple(2.0 ** (-8.0 * (h + 1) / nh) for h in range(nh))
    return pl.pallas_call(
        functools.partial(_diff_attn_kernel, t=t, slopes=slopes, out_scale=1.0 - lambda_init),
        grid=(bsz, nh, s // t),
        in_specs=[
            pl.BlockSpec((None, t, LANES), lambda b, h, i: (b, i, h)),
            pl.BlockSpec((None, s, LANES), lambda b, h, i: (b, 0, nh + h)),
            pl.BlockSpec((None, s, LANES), lambda b, h, i: (b, 0, 2 * nh + h)),
            pl.BlockSpec((8, LANES), lambda b, h, i: (0, 0)),
            pl.BlockSpec((1, LANES), lambda b, h, i: (0, 0)),
        ],
        out_specs=pl.BlockSpec((None, t, LANES), lambda b, h, i: (b, i, h)),
        out_shape=jax.ShapeDtypeStruct((bsz, s, nh * LANES), BF16),
        scratch_shapes=[pltpu.VMEM((8, LANES), F32)],
        compiler_params=_cparams(("arbitrary", "arbitrary", "arbitrary")),
        name="diff_attn",
    )(proj, proj, proj, lam, subln.reshape(1, LANES))


def _sb_attn_kernel(q_ref, k_ref, v_ref, o_ref, *, t):
    i = pl.program_id(2)
    qa, qb = _split_heads(q_ref[...])
    r = lax.broadcasted_iota(jnp.int32, (t, t), 0)
    c = lax.broadcasted_iota(jnp.int32, (t, t), 1)
    strict = c < r
    after = (r > c).astype(BF16)

    def chunk(j, ca, cb, acc, masked):
        start = pl.multiple_of(j * t, t)
        k = k_ref[pl.ds(start, t), :]
        va, vb = _split_heads(v_ref[pl.ds(start, t), :])

        def one(qm, carry, vm):
            z = _dot_nt(qm, k)
            soft = jnp.log(1.0 + jnp.exp(-jnp.abs(z)))
            log_beta = jnp.minimum(z, 0.0) - soft
            x = log_beta - z
            if masked:
                x = jnp.where(strict, x, 0.0)
            xh = x.astype(BF16)
            xl = (x - xh.astype(F32)).astype(BF16)
            inner = (jnp.dot(xh, after, preferred_element_type=F32)
                     + jnp.dot(xl, after, preferred_element_type=F32))
            att = jnp.exp(log_beta + (inner + carry))
            if masked:
                att = jnp.where(strict, att, 0.0)
            carry = carry + jnp.sum(x, axis=-1, keepdims=True)
            return carry, jnp.dot(att.astype(BF16), vm, preferred_element_type=F32)

        ca, pa = one(qa, ca, va)
        cb, pb = one(qb, cb, vb)
        return ca, cb, acc + pa + pb

    zero_col = jnp.zeros((t, 1), F32)
    ca, cb, acc = chunk(i, zero_col, zero_col, jnp.zeros((t, LANES), F32), True)

    def alive(ca, cb):
        return jnp.max(jnp.maximum(ca, cb)) > SB_DEAD_LOG

    def cond(state):
        j, _, _, _, live = state
        return jnp.logical_and(j >= 0, live)

    def body(state):
        j, ca, cb, acc, _ = state
        ca, cb, acc = chunk(j, ca, cb, acc, False)
        return j - 1, ca, cb, acc, alive(ca, cb)

    _, _, _, acc, _ = lax.while_loop(cond, body, (i - 1, ca, cb, acc, alive(ca, cb)))
    o_ref[...] = acc


def _sb_attn(proj):
    bsz, s, _ = proj.shape
    t = SB_T
    pairs = SB_HEADS * HEAD_DIM // LANES
    base = 3 * DIFF_HEADS
    return pl.pallas_call(
        functools.partial(_sb_attn_kernel, t=t),
        grid=(bsz, pairs, s // t),
        in_specs=[
            pl.BlockSpec((None, t, LANES), lambda b, p, i: (b, i, base + p)),
            pl.BlockSpec((None, s, LANES), lambda b, p, i: (b, 0, base + pairs + p)),
            pl.BlockSpec((None, s, LANES), lambda b, p, i: (b, 0, base + 2 * pairs + p)),
        ],
        out_specs=pl.BlockSpec((None, t, LANES), lambda b, p, i: (b, i, p)),
        out_shape=jax.ShapeDtypeStruct((bsz, s, pairs * LANES), F32),
        compiler_params=_cparams(("parallel", "parallel", "arbitrary")),
        name="sb_attn",
    )(proj, proj, proj)


def _mix_out_kernel(x_ref, od_ref, os_ref, gt_ref, beta_ref, gpost_ref, wd_ref, ws_ref, o_ref):
    osn = _rms(os_ref[...], beta_ref[...]).astype(BF16)
    y = (jnp.dot(od_ref[...], wd_ref[...], preferred_element_type=F32)
         + jnp.dot(osn, ws_ref[...], preferred_element_type=F32))
    o_ref[...] = x_ref[...] + MIX_RES_WEIGHT * gt_ref[...] * _rms(y, gpost_ref[...])


def _mix_out(x, o_d, o_s, mod4, k_gate, sb_beta, g_post, w_out):
    bsz, s, d = x.shape
    dd, ds = o_d.shape[-1], o_s.shape[-1]
    tm = FFN_TM
    w_bf = w_out.astype(BF16)
    return pl.pallas_call(
        _mix_out_kernel,
        grid=(bsz, s // tm),
        in_specs=[
            pl.BlockSpec((None, tm, d), lambda b, i: (b, i, 0)),
            pl.BlockSpec((None, tm, dd), lambda b, i: (b, i, 0)),
            pl.BlockSpec((None, tm, ds), lambda b, i: (b, i, 0)),
            pl.BlockSpec((None, None, 1, d), lambda b, i: (b, k_gate, 0, 0)),
            _resident((1, ds), lambda b, i: (0, 0)),
            _resident((1, d), lambda b, i: (0, 0)),
            _resident((dd, d), lambda b, i: (0, 0)),
            _resident((ds, d), lambda b, i: (0, 0)),
        ],
        out_specs=pl.BlockSpec((None, tm, d), lambda b, i: (b, i, 0)),
        out_shape=jax.ShapeDtypeStruct(x.shape, F32),
        compiler_params=_cparams(("parallel", "parallel")),
        name="mix_out",
    )(x, o_d, o_s, mod4, sb_beta.reshape(1, ds), g_post.reshape(1, d), w_bf[:dd], w_bf[dd:])


def kernel(x, c, w_ada, b_ada, ffn1_g_pre, ffn1_g_post, ffn1_w_gate, ffn1_w_up, ffn1_w_down, mix_g_pre, mix_g_post, w_in, w_out, lam_q1, lam_k1, lam_q2, lam_k2, diff_subln, sb_beta, ffn2_g_pre, ffn2_g_post, ffn2_w_gate, ffn2_w_up, ffn2_w_down):
    bsz, _, d = x.shape
    depth = w_ada.shape[0]
    for l in range(depth):
        lambda_init = 0.8 - 0.6 * math.exp(-0.3 * l)
        lam_params = jnp.stack([lam_q1[l], lam_k1[l], lam_q2[l], lam_k2[l]])
        mod, lam = _adaln(c, w_ada[l], b_ada[l], lam_params, lambda_init)
        mod4 = mod.reshape(bsz, N_MOD, 1, d)
        x = _ffn(x, mod4, 0, ffn1_g_pre[l], ffn1_g_post[l], ffn1_w_gate[l], ffn1_w_up[l], ffn1_w_down[l])
        proj = _mix_in(x, mod4, 3, mix_g_pre[l], w_in[l])
        o_d = _diff_attn(proj, lam, diff_subln[l], lambda_init)
        o_s = _sb_attn(proj)
        x = _mix_out(x, o_d, o_s, mod4, 5, sb_beta[l], mix_g_post[l], w_out[l])
        x = _ffn(x, mod4, 6, ffn2_g_pre[l], ffn2_g_post[l], ffn2_w_gate[l], ffn2_w_up[l], ffn2_w_down[l])
    return x
```

```python
import functools
import math

import jax
import jax.numpy as jnp
from jax import lax
from jax.experimental import pallas as pl
from jax.experimental.pallas import tpu as pltpu

F32 = jnp.float32
BF16 = jnp.bfloat16

HEAD_DIM = 64
DIFF_HEADS = 4
SB_HEADS = 8
N_MOD = 9
RMS_EPS = 1e-6
FFN_RES_WEIGHT = 0.5
MIX_RES_WEIGHT = 1.0

LANES = 128
VMEM_LIMIT_BYTES = 56 * 1024 * 1024

ADA_TN = 1024
FFN_TM = 512
FFN_TF = 256
DIFF_T = 512
SB_T = 256

LOG2E = math.log2(math.e)
DIFF_Q_SCALE = LOG2E * HEAD_DIM ** -0.5
SB_Q_SCALE = HEAD_DIM ** -0.5

SB_DEAD_LOG = -110.0
DIFF_DEAD_LOG2 = -150.0
DIFF_EXP_HEADROOM = 100.0
NORM_BOUND_MARGIN = 1.01
NORM_BOUND_SLACK = 1.0


def _cparams(semantics):
    return pltpu.CompilerParams(dimension_semantics=semantics, vmem_limit_bytes=VMEM_LIMIT_BYTES)


def _resident(block_shape, index_map):
    return pl.BlockSpec(block_shape, index_map, pipeline_mode=pl.Buffered(1))


def _sigmoid(x):
    return 1.0 / (1.0 + jnp.exp(-x))


def _rms(x, gain):
    ms = jnp.mean(x * x, axis=-1, keepdims=True)
    return x * lax.rsqrt(ms + RMS_EPS) * gain


def _adaln_kernel(c_ref, w_ref, b_ref, lamp_ref, mod_ref, lam_ref, *, lambda_init):
    c = c_ref[...]
    s = c * _sigmoid(c)
    mod_ref[...] = jnp.dot(s, w_ref[...], preferred_element_type=F32,
                           precision=lax.Precision.HIGHEST) + b_ref[...]
    lp = lamp_ref[...]
    d1 = jnp.sum(lp[0:1] * lp[1:2], axis=-1, keepdims=True)
    d2 = jnp.sum(lp[2:3] * lp[3:4], axis=-1, keepdims=True)
    lam = jnp.exp(d1) - jnp.exp(d2) + lambda_init
    lam_ref[...] = jnp.broadcast_to(lam, lam_ref.shape)


def _adaln(c, w_ada, b_ada, lam_params, lambda_init):
    bsz, d = c.shape
    n = w_ada.shape[1]
    tn = ADA_TN
    return pl.pallas_call(
        functools.partial(_adaln_kernel, lambda_init=lambda_init),
        grid=(n // tn,),
        in_specs=[
            pl.BlockSpec((bsz, d), lambda j: (0, 0)),
            pl.BlockSpec((d, tn), lambda j: (0, j)),
            pl.BlockSpec((1, tn), lambda j: (0, j)),
            pl.BlockSpec(lam_params.shape, lambda j: (0, 0)),
        ],
        out_specs=[
            pl.BlockSpec((bsz, tn), lambda j: (0, j)),
            pl.BlockSpec((8, LANES), lambda j: (0, 0)),
        ],
        out_shape=[
            jax.ShapeDtypeStruct((bsz, n), F32),
            jax.ShapeDtypeStruct((8, LANES), F32),
        ],
        compiler_params=_cparams(("arbitrary",)),
        name="adaln",
    )(c, w_ada, b_ada.reshape(1, n), lam_params)


def _modulated_norm(x, g_pre, scale, shift):
    return _rms(x, g_pre) * (1.0 + scale) + shift


def _ffn_kernel(x_ref, sh_ref, sc_ref, gt_ref, gpre_ref, gpost_ref, wg_ref, wu_ref, wd_ref,
                o_ref, h_scr, a_scr, *, tf):
    x = x_ref[...]
    h_scr[...] = _modulated_norm(x, gpre_ref[...], sc_ref[...], sh_ref[...]).astype(BF16)
    d_ff = wg_ref.shape[1]
    for j in range(d_ff // tf):
        cols = slice(j * tf, (j + 1) * tf)
        h = h_scr[...]
        g = jnp.dot(h, wg_ref[:, cols], preferred_element_type=F32)
        u = jnp.dot(h, wu_ref[:, cols], preferred_element_type=F32)
        a_scr[:, cols] = (g * _sigmoid(g) * u).astype(BF16)
    y = jnp.dot(a_scr[...], wd_ref[...], preferred_element_type=F32)
    o_ref[...] = x + FFN_RES_WEIGHT * gt_ref[...] * _rms(y, gpost_ref[...])


def _ffn(x, mod4, k0, g_pre, g_post, w_gate, w_up, w_down):
    bsz, s, d = x.shape
    d_ff = w_gate.shape[1]
    tm, tf = FFN_TM, FFN_TF
    mod_spec = lambda k: pl.BlockSpec((None, None, 1, d), lambda b, i: (b, k, 0, 0))
    row = lambda: _resident((1, d), lambda b, i: (0, 0))
    return pl.pallas_call(
        functools.partial(_ffn_kernel, tf=tf),
        grid=(bsz, s // tm),
        in_specs=[
            pl.BlockSpec((None, tm, d), lambda b, i: (b, i, 0)),
            mod_spec(k0), mod_spec(k0 + 1), mod_spec(k0 + 2),
            row(), row(),
            _resident((d, d_ff), lambda b, i: (0, 0)),
            _resident((d, d_ff), lambda b, i: (0, 0)),
            _resident((d_ff, d), lambda b, i: (0, 0)),
        ],
        out_specs=pl.BlockSpec((None, tm, d), lambda b, i: (b, i, 0)),
        out_shape=jax.ShapeDtypeStruct(x.shape, F32),
        scratch_shapes=[pltpu.VMEM((tm, d), BF16), pltpu.VMEM((tm, d_ff), BF16)],
        compiler_params=_cparams(("parallel", "parallel")),
        name="ffn",
    )(x, mod4, mod4, mod4, g_pre.reshape(1, d), g_post.reshape(1, d),
      w_gate.astype(BF16), w_up.astype(BF16), w_down.astype(BF16))


def _mix_in_kernel(x_ref, sh_ref, sc_ref, gpre_ref, w_ref, o_ref, h_scr, *, group, group_scale):
    h_scr[...] = _modulated_norm(x_ref[...], gpre_ref[...], sc_ref[...], sh_ref[...]).astype(BF16)
    for g in range(w_ref.shape[1] // group):
        cols = slice(g * group, (g + 1) * group)
        p = jnp.dot(h_scr[...], w_ref[:, cols], preferred_element_type=F32)
        if g in group_scale:
            p = p * group_scale[g]
        o_ref[:, cols] = p.astype(BF16)


def _mix_in(x, mod4, k0, g_pre, w_in):
    bsz, s, d = x.shape
    n = w_in.shape[1]
    group = n // 6
    tm = FFN_TM
    mod_spec = lambda k: pl.BlockSpec((None, None, 1, d), lambda b, i: (b, k, 0, 0))
    return pl.pallas_call(
        functools.partial(_mix_in_kernel, group=group,
                          group_scale={0: DIFF_Q_SCALE, 3: SB_Q_SCALE}),
        grid=(bsz, s // tm),
        in_specs=[
            pl.BlockSpec((None, tm, d), lambda b, i: (b, i, 0)),
            mod_spec(k0), mod_spec(k0 + 1),
            _resident((1, d), lambda b, i: (0, 0)),
            _resident((d, n), lambda b, i: (0, 0)),
        ],
        out_specs=pl.BlockSpec((None, tm, n), lambda b, i: (b, i, 0)),
        out_shape=jax.ShapeDtypeStruct((bsz, s, n), BF16),
        scratch_shapes=[pltpu.VMEM((tm, d), BF16)],
        compiler_params=_cparams(("parallel", "parallel")),
        name="mix_in",
    )(x, mod4, mod4, g_pre.reshape(1, d), w_in.astype(BF16))


def _dot_nt(a, b):
    return lax.dot_general(a, b, (((1,), (1,)), ((), ())), preferred_element_type=F32)


def _split_heads(x):
    lane = lax.broadcasted_iota(jnp.int32, x.shape, 1)
    zero = jnp.zeros_like(x)
    return jnp.where(lane < HEAD_DIM, x, zero), jnp.where(lane >= HEAD_DIM, x, zero)


def _diff_attn_kernel(q_ref, k_ref, v_ref, lam_ref, subln_ref, o_ref, kn_scr, m_scr, l_scr, a_scr,
                      *, t, slopes, out_scale):
    head = pl.program_id(1)
    i = pl.program_id(2)
    slope = jnp.float32(slopes[-1])
    for hh in range(len(slopes) - 2, -1, -1):
        slope = jnp.where(head == hh, jnp.float32(slopes[hh]), slope)
    slope = slope * LOG2E

    maps = _split_heads(q_ref[...])
    r = lax.broadcasted_iota(jnp.int32, (t, t), 0)
    c = lax.broadcasted_iota(jnp.int32, (t, t), 1)
    rel = (r - c).astype(F32) * slope
    causal = c <= r
    lane_tiles = t // LANES

    def lane_partial_sum(p):
        parts = [p[:, n * LANES:(n + 1) * LANES] for n in range(lane_tiles)]
        return functools.reduce(lambda a, b: a + b, parts)

    def chunk(j, *, diagonal, new_max):
        start = pl.multiple_of(j * t, t)
        k = k_ref[pl.ds(start, t), :]
        v = v_ref[pl.ds(start, t), :]
        off = slope * ((i - j) * t).astype(F32)
        for mp, qm in enumerate(maps):
            s = _dot_nt(qm, k) - rel
            if diagonal:
                s = jnp.where(causal, s, -jnp.inf)
                m_new = jnp.broadcast_to(jnp.max(s, axis=-1, keepdims=True) - off, (t, LANES))
                m_scr[mp] = m_new
                l_prev = a_prev = None
            elif new_max:
                m = m_scr[mp]
                m_new = jnp.maximum(m, jnp.max(s, axis=-1, keepdims=True) - off)
                m_scr[mp] = m_new
                alpha = jnp.exp2(m - m_new)
                l_prev, a_prev = alpha * l_scr[mp], alpha * a_scr[mp]
            else:
                m_new = m_scr[mp]
                l_prev, a_prev = l_scr[mp], a_scr[mp]
            p = jnp.exp2(s - jnp.tile(m_new + off, (1, lane_tiles)))
            l_new = lane_partial_sum(p)
            a_new = jnp.dot(p.astype(BF16), v, preferred_element_type=F32)
            l_scr[mp] = l_new if diagonal else l_prev + l_new
            a_scr[mp] = a_new if diagonal else a_prev + a_new

    sel_r = lax.broadcasted_iota(jnp.int32, (LANES, LANES), 0)
    sel_c = lax.broadcasted_iota(jnp.int32, (LANES, LANES), 1)
    sel = jnp.where(sel_c == lax.shift_right_logical(sel_r, HEAD_DIM.bit_length() - 1), 1.0, 0.0).astype(BF16)

    def max_sq_norm(x):
        xf = x.astype(F32)
        n2 = jnp.dot((xf * xf).astype(BF16), sel, preferred_element_type=F32)
        return jnp.max(n2, axis=0, keepdims=True)

    @pl.when(i == 0)
    def _():
        def body(jj, best):
            kk = k_ref[pl.ds(pl.multiple_of(jj * t, t), t), :]
            return jnp.maximum(best, max_sq_norm(kk))
        kn2 = lax.fori_loop(0, k_ref.shape[0] // t, body, jnp.zeros((1, LANES), F32))
        kn_scr[...] = jnp.broadcast_to(kn2, kn_scr.shape)

    bound = jnp.sqrt(max_sq_norm(q_ref[...]) * kn_scr[0:1, :]) * NORM_BOUND_MARGIN + NORM_BOUND_SLACK

    def slack():
        gaps = [bound[:, mp:mp + 1] - jnp.min(m_scr[mp], axis=0, keepdims=True)[:, 0:1]
                for mp in range(len(maps))]
        return functools.reduce(jnp.maximum, gaps)[0, 0]

    chunk(i, diagonal=True, new_max=True)

    def cond(st):
        j, slk = st
        nearest = ((i - j - 1) * t + 1).astype(F32)
        return jnp.logical_and(j >= 0, slk - slope * nearest >= DIFF_DEAD_LOG2)

    def keep_max(j, slk):
        chunk(j, diagonal=False, new_max=False)
        return slk

    def update_max(j, slk):
        chunk(j, diagonal=False, new_max=True)
        return slack()

    def body(st):
        j, slk = st
        return j - 1, lax.cond(slk <= DIFF_EXP_HEADROOM, keep_max, update_max, j, slk)

    lax.while_loop(cond, body, (i - 1, slack()))

    lam = lam_ref[0:1, :]
    l0 = jnp.sum(l_scr[0], axis=-1, keepdims=True)
    l1 = jnp.sum(l_scr[1], axis=-1, keepdims=True)
    o = a_scr[0] / l0 - lam * (a_scr[1] / l1)
    o_ref[...] = (_rms(o, subln_ref[...]) * out_scale).astype(o_ref.dtype)


def _diff_attn(proj, lam, subln, lambda_init):
    bsz, s, _ = proj.shape
    t = DIFF_T
    nh = DIFF_HEADS
    slopes = tuple(2.0 ** (-8.0 * (h + 1) / nh) for h in range(nh))
    return pl.pallas_call(
        functools.partial(_diff_attn_kernel, t=t, slopes=slopes, out_scale=1.0 - lambda_init),
        grid=(bsz, nh, s // t),
        in_specs=[
            pl.BlockSpec((None, t, LANES), lambda b, h, i: (b, i, h)),
            pl.BlockSpec((None, s, LANES), lambda b, h, i: (b, 0, nh + h)),
            pl.BlockSpec((None, s, LANES), lambda b, h, i: (b, 0, 2 * nh + h)),
            pl.BlockSpec((8, LANES), lambda b, h, i: (0, 0)),
            pl.BlockSpec((1, LANES), lambda b, h, i: (0, 0)),
        ],
        out_specs=pl.BlockSpec((None, t, LANES), lambda b, h, i: (b, i, h)),
        out_shape=jax.ShapeDtypeStruct((bsz, s, nh * LANES), BF16),
        scratch_shapes=[pltpu.VMEM((8, LANES), F32)] + [pltpu.VMEM((2, t, LANES), F32)] * 3,
        compiler_params=_cparams(("arbitrary", "arbitrary", "arbitrary")),
        name="diff_attn",
    )(proj, proj, proj, lam, subln.reshape(1, LANES))


def _sb_attn_kernel(q_ref, k_ref, v_ref, o_ref, *, t):
    i = pl.program_id(2)
    qa, qb = _split_heads(q_ref[...])
    r = lax.broadcasted_iota(jnp.int32, (t, t), 0)
    c = lax.broadcasted_iota(jnp.int32, (t, t), 1)
    strict = c < r
    after = (r > c).astype(BF16)

    def chunk(j, ca, cb, acc, masked):
        start = pl.multiple_of(j * t, t)
        k = k_ref[pl.ds(start, t), :]
        va, vb = _split_heads(v_ref[pl.ds(start, t), :])

        def one(qm, carry, vm):
            z = _dot_nt(qm, k)
            soft = jnp.log(1.0 + jnp.exp(-jnp.abs(z)))
            log_beta = jnp.minimum(z, 0.0) - soft
            x = log_beta - z
            if masked:
                x = jnp.where(strict, x, 0.0)
            xh = x.astype(BF16)
            xl = (x - xh.astype(F32)).astype(BF16)
            inner = (jnp.dot(xh, after, preferred_element_type=F32)
                     + jnp.dot(xl, after, preferred_element_type=F32))
            att = jnp.exp(log_beta + (inner + carry))
            if masked:
                att = jnp.where(strict, att, 0.0)
            carry = carry + jnp.sum(x, axis=-1, keepdims=True)
            return carry, jnp.dot(att.astype(BF16), vm, preferred_element_type=F32)

        ca, pa = one(qa, ca, va)
        cb, pb = one(qb, cb, vb)
        return ca, cb, acc + pa + pb

    zero_col = jnp.zeros((t, 1), F32)
    ca, cb, acc = chunk(i, zero_col, zero_col, jnp.zeros((t, LANES), F32), True)

    def alive(ca, cb):
        return jnp.max(jnp.maximum(ca, cb)) > SB_DEAD_LOG

    def cond(state):
        j, _, _, _, live = state
        return jnp.logical_and(j >= 0, live)

    def body(state):
        j, ca, cb, acc, _ = state
        ca, cb, acc = chunk(j, ca, cb, acc, False)
        return j - 1, ca, cb, acc, alive(ca, cb)

    _, _, _, acc, _ = lax.while_loop(cond, body, (i - 1, ca, cb, acc, alive(ca, cb)))
    o_ref[...] = acc


def _sb_attn(proj):
    bsz, s, _ = proj.shape
    t = SB_T
    pairs = SB_HEADS * HEAD_DIM // LANES
    base = 3 * DIFF_HEADS
    return pl.pallas_call(
        functools.partial(_sb_attn_kernel, t=t),
        grid=(bsz, pairs, s // t),
        in_specs=[
            pl.BlockSpec((None, t, LANES), lambda b, p, i: (b, i, base + p)),
            pl.BlockSpec((None, s, LANES), lambda b, p, i: (b, 0, base + pairs + p)),
            pl.BlockSpec((None, s, LANES), lambda b, p, i: (b, 0, base + 2 * pairs + p)),
        ],
        out_specs=pl.BlockSpec((None, t, LANES), lambda b, p, i: (b, i, p)),
        out_shape=jax.ShapeDtypeStruct((bsz, s, pairs * LANES), F32),
        compiler_params=_cparams(("parallel", "parallel", "arbitrary")),
        name="sb_attn",
    )(proj, proj, proj)


def _mix_out_kernel(x_ref, od_ref, os_ref, gt_ref, beta_ref, gpost_ref, wd_ref, ws_ref, o_ref):
    osn = _rms(os_ref[...], beta_ref[...]).astype(BF16)
    y = (jnp.dot(od_ref[...], wd_ref[...], preferred_element_type=F32)
         + jnp.dot(osn, ws_ref[...], preferred_element_type=F32))
    o_ref[...] = x_ref[...] + MIX_RES_WEIGHT * gt_ref[...] * _rms(y, gpost_ref[...])


def _mix_out(x, o_d, o_s, mod4, k_gate, sb_beta, g_post, w_out):
    bsz, s, d = x.shape
    dd, ds = o_d.shape[-1], o_s.shape[-1]
    tm = FFN_TM
    w_bf = w_out.astype(BF16)
    return pl.pallas_call(
        _mix_out_kernel,
        grid=(bsz, s // tm),
        in_specs=[
            pl.BlockSpec((None, tm, d), lambda b, i: (b, i, 0)),
            pl.BlockSpec((None, tm, dd), lambda b, i: (b, i, 0)),
            pl.BlockSpec((None, tm, ds), lambda b, i: (b, i, 0)),
            pl.BlockSpec((None, None, 1, d), lambda b, i: (b, k_gate, 0, 0)),
            _resident((1, ds), lambda b, i: (0, 0)),
            _resident((1, d), lambda b, i: (0, 0)),
            _resident((dd, d), lambda b, i: (0, 0)),
            _resident((ds, d), lambda b, i: (0, 0)),
        ],
        out_specs=pl.BlockSpec((None, tm, d), lambda b, i: (b, i, 0)),
        out_shape=jax.ShapeDtypeStruct(x.shape, F32),
        compiler_params=_cparams(("parallel", "parallel")),
        name="mix_out",
    )(x, o_d, o_s, mod4, sb_beta.reshape(1, ds), g_post.reshape(1, d), w_bf[:dd], w_bf[dd:])


def kernel(x, c, w_ada, b_ada, ffn1_g_pre, ffn1_g_post, ffn1_w_gate, ffn1_w_up, ffn1_w_down, mix_g_pre, mix_g_post, w_in, w_out, lam_q1, lam_k1, lam_q2, lam_k2, diff_subln, sb_beta, ffn2_g_pre, ffn2_g_post, ffn2_w_gate, ffn2_w_up, ffn2_w_down):
    bsz, _, d = x.shape
    depth = w_ada.shape[0]
    for l in range(depth):
        lambda_init = 0.8 - 0.6 * math.exp(-0.3 * l)
        lam_params = jnp.stack([lam_q1[l], lam_k1[l], lam_q2[l], lam_k2[l]])
        mod, lam = _adaln(c, w_ada[l], b_ada[l], lam_params, lambda_init)
        mod4 = mod.reshape(bsz, N_MOD, 1, d)
        x = _ffn(x, mod4, 0, ffn1_g_pre[l], ffn1_g_post[l], ffn1_w_gate[l], ffn1_w_up[l], ffn1_w_down[l])
        proj = _mix_in(x, mod4, 3, mix_g_pre[l], w_in[l])
        o_d = _diff_attn(proj, lam, diff_subln[l], lambda_init)
        o_s = _sb_attn(proj)
        x = _mix_out(x, o_d, o_s, mod4, 5, sb_beta[l], mix_g_post[l], w_out[l])
        x = _ffn(x, mod4, 6, ffn2_g_pre[l], ffn2_g_post[l], ffn2_w_gate[l], ffn2_w_up[l], ffn2_w_down[l])
    return x
```

```python
import functools
import math

import jax
import jax.numpy as jnp
from jax import lax
from jax.experimental import pallas as pl
from jax.experimental.pallas import tpu as pltpu

F32 = jnp.float32
BF16 = jnp.bfloat16

HEAD_DIM = 64
DIFF_HEADS = 4
SB_HEADS = 8
N_MOD = 9
RMS_EPS = 1e-6
FFN_RES_WEIGHT = 0.5
MIX_RES_WEIGHT = 1.0

LANES = 128
VMEM_LIMIT_BYTES = 56 * 1024 * 1024

ADA_TN = 1024
FFN_TM = 512
FFN_TF = 256
DIFF_T = 512
SB_T = 256

LOG2E = math.log2(math.e)
DIFF_Q_SCALE = LOG2E * HEAD_DIM ** -0.5
SB_Q_SCALE = HEAD_DIM ** -0.5

SB_DEAD_LOG = -110.0
DIFF_DEAD_LOG2 = -150.0
DIFF_EXP_HEADROOM = 100.0
NORM_BOUND_MARGIN = 1.01
NORM_BOUND_SLACK = 1.0


def _cparams(semantics):
    return pltpu.CompilerParams(dimension_semantics=semantics, vmem_limit_bytes=VMEM_LIMIT_BYTES)


def _resident(block_shape, index_map):
    return pl.BlockSpec(block_shape, index_map, pipeline_mode=pl.Buffered(1))


def _sigmoid(x):
    return 1.0 / (1.0 + jnp.exp(-x))


def _rms(x, gain):
    ms = jnp.mean(x * x, axis=-1, keepdims=True)
    return x * lax.rsqrt(ms + RMS_EPS) * gain


def _adaln_kernel(c_ref, w_ref, b_ref, lamp_ref, mod_ref, lam_ref, *, lambda_init):
    c = c_ref[...]
    s = c * _sigmoid(c)
    mod_ref[...] = jnp.dot(s, w_ref[...], preferred_element_type=F32,
                           precision=lax.Precision.HIGHEST) + b_ref[...]
    lp = lamp_ref[...]
    d1 = jnp.sum(lp[0:1] * lp[1:2], axis=-1, keepdims=True)
    d2 = jnp.sum(lp[2:3] * lp[3:4], axis=-1, keepdims=True)
    lam = jnp.exp(d1) - jnp.exp(d2) + lambda_init
    lam_ref[...] = jnp.broadcast_to(lam, lam_ref.shape)


def _adaln(c, w_ada, b_ada, lam_params, lambda_init):
    bsz, d = c.shape
    n = w_ada.shape[1]
    tn = ADA_TN
    return pl.pallas_call(
        functools.partial(_adaln_kernel, lambda_init=lambda_init),
        grid=(n // tn,),
        in_specs=[
            pl.BlockSpec((bsz, d), lambda j: (0, 0)),
            pl.BlockSpec((d, tn), lambda j: (0, j)),
            pl.BlockSpec((1, tn), lambda j: (0, j)),
            pl.BlockSpec(lam_params.shape, lambda j: (0, 0)),
        ],
        out_specs=[
            pl.BlockSpec((bsz, tn), lambda j: (0, j)),
            pl.BlockSpec((8, LANES), lambda j: (0, 0)),
        ],
        out_shape=[
            jax.ShapeDtypeStruct((bsz, n), F32),
            jax.ShapeDtypeStruct((8, LANES), F32),
        ],
        compiler_params=_cparams(("arbitrary",)),
        name="adaln",
    )(c, w_ada, b_ada.reshape(1, n), lam_params)


def _modulated_norm(x, g_pre, scale, shift):
    return _rms(x, g_pre) * (1.0 + scale) + shift


def _ffn_kernel(x_ref, sh_ref, sc_ref, gt_ref, gpre_ref, gpost_ref, wg_ref, wu_ref, wd_ref,
                o_ref, h_scr, a_scr, *, tf):
    x = x_ref[...]
    h_scr[...] = _modulated_norm(x, gpre_ref[...], sc_ref[...], sh_ref[...]).astype(BF16)
    d_ff = wg_ref.shape[1]
    for j in range(d_ff // tf):
        cols = slice(j * tf, (j + 1) * tf)
        h = h_scr[...]
        g = jnp.dot(h, wg_ref[:, cols], preferred_element_type=F32)
        u = jnp.dot(h, wu_ref[:, cols], preferred_element_type=F32)
        a_scr[:, cols] = (g * _sigmoid(g) * u).astype(BF16)
    y = jnp.dot(a_scr[...], wd_ref[...], preferred_element_type=F32)
    o_ref[...] = x + FFN_RES_WEIGHT * gt_ref[...] * _rms(y, gpost_ref[...])


def _ffn(x, mod4, k0, g_pre, g_post, w_gate, w_up, w_down):
    bsz, s, d = x.shape
    d_ff = w_gate.shape[1]
    tm, tf = FFN_TM, FFN_TF
    mod_spec = lambda k: pl.BlockSpec((None, None, 1, d), lambda b, i: (b, k, 0, 0))
    row = lambda: _resident((1, d), lambda b, i: (0, 0))
    return pl.pallas_call(
        functools.partial(_ffn_kernel, tf=tf),
        grid=(bsz, s // tm),
        in_specs=[
            pl.BlockSpec((None, tm, d), lambda b, i: (b, i, 0)),
            mod_spec(k0), mod_spec(k0 + 1), mod_spec(k0 + 2),
            row(), row(),
            _resident((d, d_ff), lambda b, i: (0, 0)),
            _resident((d, d_ff), lambda b, i: (0, 0)),
            _resident((d_ff, d), lambda b, i: (0, 0)),
        ],
        out_specs=pl.BlockSpec((None, tm, d), lambda b, i: (b, i, 0)),
        out_shape=jax.ShapeDtypeStruct(x.shape, F32),
        scratch_shapes=[pltpu.VMEM((tm, d), BF16), pltpu.VMEM((tm, d_ff), BF16)],
        compiler_params=_cparams(("parallel", "parallel")),
        name="ffn",
    )(x, mod4, mod4, mod4, g_pre.reshape(1, d), g_post.reshape(1, d),
      w_gate.astype(BF16), w_up.astype(BF16), w_down.astype(BF16))


def _mix_in_kernel(x_ref, sh_ref, sc_ref, gpre_ref, w_ref, o_ref, h_scr, *, group, group_scale):
    h_scr[...] = _modulated_norm(x_ref[...], gpre_ref[...], sc_ref[...], sh_ref[...]).astype(BF16)
    for g in range(w_ref.shape[1] // group):
        cols = slice(g * group, (g + 1) * group)
        p = jnp.dot(h_scr[...], w_ref[:, cols], preferred_element_type=F32)
        if g in group_scale:
            p = p * group_scale[g]
        o_ref[:, cols] = p.astype(BF16)


def _mix_in(x, mod4, k0, g_pre, w_in):
    bsz, s, d = x.shape
    n = w_in.shape[1]
    group = n // 6
    tm = FFN_TM
    mod_spec = lambda k: pl.BlockSpec((None, None, 1, d), lambda b, i: (b, k, 0, 0))
    return pl.pallas_call(
        functools.partial(_mix_in_kernel, group=group,
                          group_scale={0: DIFF_Q_SCALE, 3: SB_Q_SCALE}),
        grid=(bsz, s // tm),
        in_specs=[
            pl.BlockSpec((None, tm, d), lambda b, i: (b, i, 0)),
            mod_spec(k0), mod_spec(k0 + 1),
            _resident((1, d), lambda b, i: (0, 0)),
            _resident((d, n), lambda b, i: (0, 0)),
        ],
        out_specs=pl.BlockSpec((None, tm, n), lambda b, i: (b, i, 0)),
        out_shape=jax.ShapeDtypeStruct((bsz, s, n), BF16),
        scratch_shapes=[pltpu.VMEM((tm, d), BF16)],
        compiler_params=_cparams(("parallel", "parallel")),
        name="mix_in",
    )(x, mod4, mod4, g_pre.reshape(1, d), w_in.astype(BF16))


def _dot_nt(a, b):
    return lax.dot_general(a, b, (((1,), (1,)), ((), ())), preferred_element_type=F32)


def _split_heads(x):
    lane = lax.broadcasted_iota(jnp.int32, x.shape, 1)
    zero = jnp.zeros_like(x)
    return jnp.where(lane < HEAD_DIM, x, zero), jnp.where(lane >= HEAD_DIM, x, zero)


def _diff_attn_kernel(q_ref, k_ref, v_ref, lam_ref, subln_ref, o_ref, kn_scr, m_scr, l_scr, a_scr,
                      *, t, slopes, out_scale):
    head = pl.program_id(1)
    i = pl.program_id(2)
    slope = jnp.float32(slopes[-1])
    for hh in range(len(slopes) - 2, -1, -1):
        slope = jnp.where(head == hh, jnp.float32(slopes[hh]), slope)
    slope = slope * LOG2E

    maps = _split_heads(q_ref[...])
    r = lax.broadcasted_iota(jnp.int32, (t, t), 0)
    c = lax.broadcasted_iota(jnp.int32, (t, t), 1)
    rel = (r - c).astype(F32) * slope
    causal = c <= r
    lane_tiles = t // LANES

    def lane_partial_sum(p):
        parts = [p[:, n * LANES:(n + 1) * LANES] for n in range(lane_tiles)]
        return functools.reduce(lambda a, b: a + b, parts)

    def chunk(j, *, diagonal, new_max):
        start = pl.multiple_of(j * t, t)
        k = k_ref[pl.ds(start, t), :]
        v = v_ref[pl.ds(start, t), :]
        off = slope * ((i - j) * t).astype(F32)
        for mp, qm in enumerate(maps):
            s = _dot_nt(qm, k) - rel
            if diagonal:
                s = jnp.where(causal, s, -jnp.inf)
                m_new = jnp.broadcast_to(jnp.max(s, axis=-1, keepdims=True) - off, (t, LANES))
                m_scr[mp] = m_new
                l_prev = a_prev = None
            elif new_max:
                m = m_scr[mp]
                m_new = jnp.maximum(m, jnp.max(s, axis=-1, keepdims=True) - off)
                m_scr[mp] = m_new
                alpha = jnp.exp2(m - m_new)
                l_prev, a_prev = alpha * l_scr[mp], alpha * a_scr[mp]
            else:
                m_new = m_scr[mp]
                l_prev, a_prev = l_scr[mp], a_scr[mp]
            p = jnp.exp2(s - jnp.tile(m_new + off, (1, lane_tiles)))
            l_new = lane_partial_sum(p)
            a_new = jnp.dot(p.astype(BF16), v, preferred_element_type=F32)
            l_scr[mp] = l_new if diagonal else l_prev + l_new
            a_scr[mp] = a_new if diagonal else a_prev + a_new

    sel_r = lax.broadcasted_iota(jnp.int32, (LANES, LANES), 0)
    sel_c = lax.broadcasted_iota(jnp.int32, (LANES, LANES), 1)
    sel = jnp.where(sel_c == lax.shift_right_logical(sel_r, HEAD_DIM.bit_length() - 1), 1.0, 0.0).astype(BF16)

    def max_sq_norm(x):
        xf = x.astype(F32)
        n2 = jnp.dot((xf * xf).astype(BF16), sel, preferred_element_type=F32)
        return jnp.max(n2, axis=0, keepdims=True)

    @pl.when(i == 0)
    def _():
        def body(jj, best):
            kk = k_ref[pl.ds(pl.multiple_of(jj * t, t), t), :]
            return jnp.maximum(best, max_sq_norm(kk))
        kn2 = lax.fori_loop(0, k_ref.shape[0] // t, body, jnp.zeros((1, LANES), F32))
        kn_scr[...] = jnp.broadcast_to(kn2, kn_scr.shape)

    bound = jnp.sqrt(max_sq_norm(q_ref[...]) * kn_scr[0:1, :]) * NORM_BOUND_MARGIN + NORM_BOUND_SLACK

    def slack():
        gaps = [bound[:, mp:mp + 1] - jnp.min(m_scr[mp], axis=0, keepdims=True)[:, 0:1]
                for mp in range(len(maps))]
        return functools.reduce(jnp.maximum, gaps)[0, 0]

    chunk(i, diagonal=True, new_max=True)

    def cond(st):
        j, slk = st
        nearest = ((i - j - 1) * t + 1).astype(F32)
        return jnp.logical_and(j >= 0, slk - slope * nearest >= DIFF_DEAD_LOG2)

    def keep_max(j, slk):
        chunk(j, diagonal=False, new_max=False)
        return slk

    def update_max(j, slk):
        chunk(j, diagonal=False, new_max=True)
        return slack()

    def body(st):
        j, slk = st
        return j - 1, lax.cond(slk <= DIFF_EXP_HEADROOM, keep_max, update_max, j, slk)

    lax.while_loop(cond, body, (i - 1, slack()))

    lam = lam_ref[0:1, :]
    l0 = jnp.sum(l_scr[0], axis=-1, keepdims=True)
    l1 = jnp.sum(l_scr[1], axis=-1, keepdims=True)
    o = a_scr[0] / l0 - lam * (a_scr[1] / l1)
    o_ref[...] = (_rms(o, subln_ref[...]) * out_scale).astype(o_ref.dtype)


def _diff_attn(proj, lam, subln, lambda_init):
    bsz, s, _ = proj.shape
    t = DIFF_T
    nh = DIFF_HEADS
    slopes = tuple(2.0 ** (-8.0 * (h + 1) / nh) for h in range(nh))
    return pl.pallas_call(
        functools.partial(_diff_attn_kernel, t=t, slopes=slopes, out_scale=1.0 - lambda_init),
        grid=(bsz, nh, s // t),
        in_specs=[
            pl.BlockSpec((None, t, LANES), lambda b, h, i: (b, i, h)),
            pl.BlockSpec((None, s, LANES), lambda b, h, i: (b, 0, nh + h)),
            pl.BlockSpec((None, s, LANES), lambda b, h, i: (b, 0, 2 * nh + h)),
            pl.BlockSpec((8, LANES), lambda b, h, i: (0, 0)),
            pl.BlockSpec((1, LANES), lambda b, h, i: (0, 0)),
        ],
        out_specs=pl.BlockSpec((None, t, LANES), lambda b, h, i: (b, i, h)),
        out_shape=jax.ShapeDtypeStruct((bsz, s, nh * LANES), BF16),
        scratch_shapes=[pltpu.VMEM((8, LANES), F32)] + [pltpu.VMEM((2, t, LANES), F32)] * 3,
        compiler_params=_cparams(("arbitrary", "arbitrary", "arbitrary")),
        name="diff_attn",
    )(proj, proj, proj, lam, subln.reshape(1, LANES))


def _sb_attn_kernel(q_ref, k_ref, v_ref, o_ref, *, t):
    i = pl.program_id(2)
    queries = _split_heads(q_ref[...])
    r = lax.broadcasted_iota(jnp.int32, (t, t), 0)
    c = lax.broadcasted_iota(jnp.int32, (t, t), 1)
    strict = c < r
    after = (r > c).astype(BF16)

    def prefix(j, *, diagonal, valid=None):
        start = pl.multiple_of(j * t, t)
        k = k_ref[pl.ds(start, t), :]
        parts = []
        for qm in queries:
            z = _dot_nt(qm, k)
            soft = jnp.log(1.0 + jnp.exp(-jnp.abs(z)))
            log_beta = jnp.minimum(z, 0.0) - soft
            x = log_beta - z
            if diagonal:
                x = jnp.where(strict, x, 0.0)
            if valid is not None:
                x = jnp.where(valid, x, 0.0)
            xh = x.astype(BF16)
            xl = (x - xh.astype(F32)).astype(BF16)
            inner = (jnp.dot(xh, after, preferred_element_type=F32)
                     + jnp.dot(xl, after, preferred_element_type=F32))
            parts.append((log_beta, inner, jnp.sum(x, axis=-1, keepdims=True)))
        return parts, _split_heads(v_ref[pl.ds(start, t), :])

    def finish(parts, values, carries, *, diagonal, valid=None):
        out, new_carries = None, []
        for (log_beta, inner, total), vm, carry in zip(parts, values, carries):
            att = jnp.exp(log_beta + (inner + carry))
            if diagonal:
                att = jnp.where(strict, att, 0.0)
            if valid is not None:
                att = jnp.where(valid, att, 0.0)
            pv = jnp.dot(att.astype(BF16), vm, preferred_element_type=F32)
            out = pv if out is None else out + pv
            new_carries.append(carry + total)
        return new_carries, out

    has_prev = i > 0
    zero_col = jnp.zeros((t, 1), F32)
    diag_parts, diag_values = prefix(i, diagonal=True)
    prev_parts, prev_values = prefix(jnp.maximum(i - 1, 0), diagonal=False, valid=has_prev)
    carries, acc = finish(diag_parts, diag_values, [zero_col, zero_col], diagonal=True)
    carries, acc_prev = finish(prev_parts, prev_values, carries, diagonal=False, valid=has_prev)
    acc = acc + acc_prev

    def alive(carries):
        return jnp.max(functools.reduce(jnp.maximum, carries)) > SB_DEAD_LOG

    def cond(state):
        return jnp.logical_and(state[0] >= 0, state[-1])

    def body(state):
        j, ca, cb, acc, _ = state
        parts, values = prefix(j, diagonal=False)
        carries, out = finish(parts, values, [ca, cb], diagonal=False)
        return j - 1, carries[0], carries[1], acc + out, alive(carries)

    state = lax.while_loop(cond, body, (i - 2, carries[0], carries[1], acc, alive(carries)))
    o_ref[...] = state[3]


def _sb_attn(proj):
    bsz, s, _ = proj.shape
    t = SB_T
    pairs = SB_HEADS * HEAD_DIM // LANES
    base = 3 * DIFF_HEADS
    return pl.pallas_call(
        functools.partial(_sb_attn_kernel, t=t),
        grid=(bsz, pairs, s // t),
        in_specs=[
            pl.BlockSpec((None, t, LANES), lambda b, p, i: (b, i, base + p)),
            pl.BlockSpec((None, s, LANES), lambda b, p, i: (b, 0, base + pairs + p)),
            pl.BlockSpec((None, s, LANES), lambda b, p, i: (b, 0, base + 2 * pairs + p)),
        ],
        out_specs=pl.BlockSpec((None, t, LANES), lambda b, p, i: (b, i, p)),
        out_shape=jax.ShapeDtypeStruct((bsz, s, pairs * LANES), F32),
        compiler_params=_cparams(("parallel", "parallel", "arbitrary")),
        name="sb_attn",
    )(proj, proj, proj)


def _mix_out_kernel(x_ref, od_ref, os_ref, gt_ref, beta_ref, gpost_ref, wd_ref, ws_ref, o_ref):
    osn = _rms(os_ref[...], beta_ref[...]).astype(BF16)
    y = (jnp.dot(od_ref[...], wd_ref[...], preferred_element_type=F32)
         + jnp.dot(osn, ws_ref[...], preferred_element_type=F32))
    o_ref[...] = x_ref[...] + MIX_RES_WEIGHT * gt_ref[...] * _rms(y, gpost_ref[...])


def _mix_out(x, o_d, o_s, mod4, k_gate, sb_beta, g_post, w_out):
    bsz, s, d = x.shape
    dd, ds = o_d.shape[-1], o_s.shape[-1]
    tm = FFN_TM
    w_bf = w_out.astype(BF16)
    return pl.pallas_call(
        _mix_out_kernel,
        grid=(bsz, s // tm),
        in_specs=[
            pl.BlockSpec((None, tm, d), lambda b, i: (b, i, 0)),
            pl.BlockSpec((None, tm, dd), lambda b, i: (b, i, 0)),
            pl.BlockSpec((None, tm, ds), lambda b, i: (b, i, 0)),
            pl.BlockSpec((None, None, 1, d), lambda b, i: (b, k_gate, 0, 0)),
            _resident((1, ds), lambda b, i: (0, 0)),
            _resident((1, d), lambda b, i: (0, 0)),
            _resident((dd, d), lambda b, i: (0, 0)),
            _resident((ds, d), lambda b, i: (0, 0)),
        ],
        out_specs=pl.BlockSpec((None, tm, d), lambda b, i: (b, i, 0)),
        out_shape=jax.ShapeDtypeStruct(x.shape, F32),
        compiler_params=_cparams(("parallel", "parallel")),
        name="mix_out",
    )(x, o_d, o_s, mod4, sb_beta.reshape(1, ds), g_post.reshape(1, d), w_bf[:dd], w_bf[dd:])


def kernel(x, c, w_ada, b_ada, ffn1_g_pre, ffn1_g_post, ffn1_w_gate, ffn1_w_up, ffn1_w_down, mix_g_pre, mix_g_post, w_in, w_out, lam_q1, lam_k1, lam_q2, lam_k2, diff_subln, sb_beta, ffn2_g_pre, ffn2_g_post, ffn2_w_gate, ffn2_w_up, ffn2_w_down):
    bsz, _, d = x.shape
    depth = w_ada.shape[0]
    for l in range(depth):
        lambda_init = 0.8 - 0.6 * math.exp(-0.3 * l)
        lam_params = jnp.stack([lam_q1[l], lam_k1[l], lam_q2[l], lam_k2[l]])
        mod, lam = _adaln(c, w_ada[l], b_ada[l], lam_params, lambda_init)
        mod4 = mod.reshape(bsz, N_MOD, 1, d)
        x = _ffn(x, mod4, 0, ffn1_g_pre[l], ffn1_g_post[l], ffn1_w_gate[l], ffn1_w_up[l], ffn1_w_down[l])
        proj = _mix_in(x, mod4, 3, mix_g_pre[l], w_in[l])
        o_d = _diff_attn(proj, lam, diff_subln[l], lambda_init)
        o_s = _sb_attn(proj)
        x = _mix_out(x, o_d, o_s, mod4, 5, sb_beta[l], mix_g_post[l], w_out[l])
        x = _ffn(x, mod4, 6, ffn2_g_pre[l], ffn2_g_post[l], ffn2_w_gate[l], ffn2_w_up[l], ffn2_w_down[l])
    return x
```

```python
import functools
import math

import jax
import jax.numpy as jnp
from jax import lax
from jax.experimental import pallas as pl
from jax.experimental.pallas import tpu as pltpu

F32 = jnp.float32
BF16 = jnp.bfloat16

HEAD_DIM = 64
DIFF_HEADS = 4
SB_HEADS = 8
N_MOD = 9
RMS_EPS = 1e-6
FFN_RES_WEIGHT = 0.5
MIX_RES_WEIGHT = 1.0

LANES = 128
VMEM_LIMIT_BYTES = 56 * 1024 * 1024

ADA_TN = 1024
FFN_TM = 512
FFN_TF = 256
DIFF_T = 512
SB_T = 256
SB_PAIRS_PER_STEP = 4

LOG2E = math.log2(math.e)
DIFF_Q_SCALE = LOG2E * HEAD_DIM ** -0.5
SB_Q_SCALE = LOG2E * HEAD_DIM ** -0.5

SB_DEAD_LOG2 = -160.0
DIFF_DEAD_LOG2 = -150.0
DIFF_EXP_HEADROOM = 100.0
NORM_BOUND_MARGIN = 1.01
NORM_BOUND_SLACK = 1.0


def _cparams(semantics):
    return pltpu.CompilerParams(dimension_semantics=semantics, vmem_limit_bytes=VMEM_LIMIT_BYTES)


def _resident(block_shape, index_map):
    return pl.BlockSpec(block_shape, index_map, pipeline_mode=pl.Buffered(1))


def _sigmoid(x):
    return 1.0 / (1.0 + jnp.exp(-x))


def _rms(x, gain):
    ms = jnp.mean(x * x, axis=-1, keepdims=True)
    return x * lax.rsqrt(ms + RMS_EPS) * gain


def _adaln_kernel(c_ref, w_ref, b_ref, lamp_ref, mod_ref, lam_ref, *, lambda_init):
    c = c_ref[...]
    s = c * _sigmoid(c)
    mod_ref[...] = jnp.dot(s, w_ref[...], preferred_element_type=F32,
                           precision=lax.Precision.HIGHEST) + b_ref[...]
    lp = lamp_ref[...]
    d1 = jnp.sum(lp[0:1] * lp[1:2], axis=-1, keepdims=True)
    d2 = jnp.sum(lp[2:3] * lp[3:4], axis=-1, keepdims=True)
    lam = jnp.exp(d1) - jnp.exp(d2) + lambda_init
    lam_ref[...] = jnp.broadcast_to(lam, lam_ref.shape)


def _adaln(c, w_ada, b_ada, lam_params, lambda_init):
    bsz, d = c.shape
    n = w_ada.shape[1]
    tn = ADA_TN
    return pl.pallas_call(
        functools.partial(_adaln_kernel, lambda_init=lambda_init),
        grid=(n // tn,),
        in_specs=[
            pl.BlockSpec((bsz, d), lambda j: (0, 0)),
            pl.BlockSpec((d, tn), lambda j: (0, j)),
            pl.BlockSpec((1, tn), lambda j: (0, j)),
            pl.BlockSpec(lam_params.shape, lambda j: (0, 0)),
        ],
        out_specs=[
            pl.BlockSpec((bsz, tn), lambda j: (0, j)),
            pl.BlockSpec((8, LANES), lambda j: (0, 0)),
        ],
        out_shape=[
            jax.ShapeDtypeStruct((bsz, n), F32),
            jax.ShapeDtypeStruct((8, LANES), F32),
        ],
        compiler_params=_cparams(("arbitrary",)),
        name="adaln",
    )(c, w_ada, b_ada.reshape(1, n), lam_params)


def _modulated_norm(x, g_pre, scale, shift):
    return _rms(x, g_pre) * (1.0 + scale) + shift


def _swiglu_sublayer(x_ref, sh, sc, gt, g_pre, g_post, wg_ref, wu_ref, wd_ref, h_scr, a_scr, tf):
    h_scr[...] = _modulated_norm(x_ref[...], g_pre, sc, sh).astype(BF16)
    d_ff = wg_ref.shape[1]
    for j in range(d_ff // tf):
        cols = slice(j * tf, (j + 1) * tf)
        h = h_scr[...]
        g = jnp.dot(h, wg_ref[:, cols], preferred_element_type=F32)
        u = jnp.dot(h, wu_ref[:, cols], preferred_element_type=F32)
        a_scr[:, cols] = (g * _sigmoid(g) * u).astype(BF16)
    y = jnp.dot(a_scr[...], wd_ref[...], preferred_element_type=F32)
    return x_ref[...] + FFN_RES_WEIGHT * gt * _rms(y, g_post)


def _ffn_kernel(x_ref, sh_ref, sc_ref, gt_ref, gpre_ref, gpost_ref, wg_ref, wu_ref, wd_ref,
                o_ref, h_scr, a_scr, *, tf):
    o_ref[...] = _swiglu_sublayer(x_ref, sh_ref[...], sc_ref[...], gt_ref[...], gpre_ref[...], gpost_ref[...],
                                  wg_ref, wu_ref, wd_ref, h_scr, a_scr, tf)


def _ffn(x, mod4, k0, g_pre, g_post, w_gate, w_up, w_down):
    bsz, s, d = x.shape
    d_ff = w_gate.shape[1]
    tm, tf = FFN_TM, FFN_TF
    mod_spec = lambda k: pl.BlockSpec((None, None, 1, d), lambda b, i: (b, k, 0, 0))
    row = lambda: _resident((1, d), lambda b, i: (0, 0))
    return pl.pallas_call(
        functools.partial(_ffn_kernel, tf=tf),
        grid=(bsz, s // tm),
        in_specs=[
            pl.BlockSpec((None, tm, d), lambda b, i: (b, i, 0)),
            mod_spec(k0), mod_spec(k0 + 1), mod_spec(k0 + 2),
            row(), row(),
            _resident((d, d_ff), lambda b, i: (0, 0)),
            _resident((d, d_ff), lambda b, i: (0, 0)),
            _resident((d_ff, d), lambda b, i: (0, 0)),
        ],
        out_specs=pl.BlockSpec((None, tm, d), lambda b, i: (b, i, 0)),
        out_shape=jax.ShapeDtypeStruct(x.shape, F32),
        scratch_shapes=[pltpu.VMEM((tm, d), BF16), pltpu.VMEM((tm, d_ff), BF16)],
        compiler_params=_cparams(("parallel", "parallel")),
        name="ffn",
    )(x, mod4, mod4, mod4, g_pre.reshape(1, d), g_post.reshape(1, d),
      w_gate.astype(BF16), w_up.astype(BF16), w_down.astype(BF16))


def _mix_in_kernel(x_ref, sh_ref, sc_ref, gpre_ref, w_ref, o_ref, h_scr, *, group, group_scale):
    h_scr[...] = _modulated_norm(x_ref[...], gpre_ref[...], sc_ref[...], sh_ref[...]).astype(BF16)
    for g in range(w_ref.shape[1] // group):
        cols = slice(g * group, (g + 1) * group)
        p = jnp.dot(h_scr[...], w_ref[:, cols], preferred_element_type=F32)
        if g in group_scale:
            p = p * group_scale[g]
        o_ref[:, cols] = p.astype(BF16)


def _mix_in(x, mod4, k0, g_pre, w_in):
    bsz, s, d = x.shape
    n = w_in.shape[1]
    group = n // 6
    tm = FFN_TM
    mod_spec = lambda k: pl.BlockSpec((None, None, 1, d), lambda b, i: (b, k, 0, 0))
    return pl.pallas_call(
        functools.partial(_mix_in_kernel, group=group,
                          group_scale={0: DIFF_Q_SCALE, 3: SB_Q_SCALE}),
        grid=(bsz, s // tm),
        in_specs=[
            pl.BlockSpec((None, tm, d), lambda b, i: (b, i, 0)),
            mod_spec(k0), mod_spec(k0 + 1),
            _resident((1, d), lambda b, i: (0, 0)),
            _resident((d, n), lambda b, i: (0, 0)),
        ],
        out_specs=pl.BlockSpec((None, tm, n), lambda b, i: (b, i, 0)),
        out_shape=jax.ShapeDtypeStruct((bsz, s, n), BF16),
        scratch_shapes=[pltpu.VMEM((tm, d), BF16)],
        compiler_params=_cparams(("parallel", "parallel")),
        name="mix_in",
    )(x, mod4, mod4, g_pre.reshape(1, d), w_in.astype(BF16))


def _dot_nt(a, b):
    return lax.dot_general(a, b, (((1,), (1,)), ((), ())), preferred_element_type=F32)


def _split_heads(x):
    lane = lax.broadcasted_iota(jnp.int32, x.shape, 1)
    zero = jnp.zeros_like(x)
    return jnp.where(lane < HEAD_DIM, x, zero), jnp.where(lane >= HEAD_DIM, x, zero)


def _diff_attn_kernel(q_ref, k_ref, v_ref, lam_ref, subln_ref, o_ref, kn_scr, rel_scr, diag_scr,
                      m_scr, l_scr, a_scr, *, t, slopes, out_scale):
    head = pl.program_id(1)
    i = pl.program_id(2)
    slope = jnp.float32(slopes[-1])
    for hh in range(len(slopes) - 2, -1, -1):
        slope = jnp.where(head == hh, jnp.float32(slopes[hh]), slope)
    slope = slope * LOG2E

    maps = _split_heads(q_ref[...])
    lane_tiles = t // LANES

    def lane_partial_sum(p):
        parts = [p[:, n * LANES:(n + 1) * LANES] for n in range(p.shape[1] // LANES)]
        return functools.reduce(lambda a, b: a + b, parts)

    def load_chunk(j):
        start = pl.multiple_of(j * t, t)
        off = slope * ((i - j) * t).astype(F32)
        return k_ref[pl.ds(start, t), :], v_ref[pl.ds(start, t), :], off

    def diagonal_chunk():
        k, v, _ = load_chunk(i)
        for mp, qm in enumerate(maps):
            s = _dot_nt(qm, k) - diag_scr[...]
            m_new = jnp.broadcast_to(jnp.max(s, axis=-1, keepdims=True), (t, LANES))
            p = jnp.exp2(s - jnp.tile(m_new, (1, lane_tiles)))
            m_scr[mp] = m_new
            l_scr[mp] = lane_partial_sum(p)
            a_scr[mp] = jnp.dot(p.astype(BF16), v, preferred_element_type=F32)

    def update_chunk(j):
        k, v, off = load_chunk(j)
        for mp, qm in enumerate(maps):
            s = _dot_nt(qm, k) - rel_scr[...]
            m = m_scr[mp]
            m_new = jnp.maximum(m, jnp.max(s, axis=-1, keepdims=True) - off)
            alpha = jnp.exp2(m - m_new)
            p = jnp.exp2(s - jnp.tile(m_new + off, (1, lane_tiles)))
            m_scr[mp] = m_new
            l_scr[mp] = alpha * l_scr[mp] + lane_partial_sum(p)
            a_scr[mp] = alpha * a_scr[mp] + jnp.dot(p.astype(BF16), v, preferred_element_type=F32)

    def chunks_keep_max(js):
        loaded = [load_chunk(j) for j in js]
        for mp, qm in enumerate(maps):
            m = m_scr[mp]
            l_new = a_new = None
            for k, v, off in loaded:
                p = jnp.exp2(_dot_nt(qm, k) - rel_scr[...] - jnp.tile(m + off, (1, lane_tiles)))
                ls = lane_partial_sum(p)
                av = jnp.dot(p.astype(BF16), v, preferred_element_type=F32)
                l_new = ls if l_new is None else l_new + ls
                a_new = av if a_new is None else a_new + av
            l_scr[mp] = l_scr[mp] + l_new
            a_scr[mp] = a_scr[mp] + a_new

    sel_r = lax.broadcasted_iota(jnp.int32, (LANES, LANES), 0)
    sel_c = lax.broadcasted_iota(jnp.int32, (LANES, LANES), 1)
    sel = jnp.where(sel_c == lax.shift_right_logical(sel_r, HEAD_DIM.bit_length() - 1), 1.0, 0.0).astype(BF16)

    def max_sq_norm(x):
        xf = x.astype(F32)
        n2 = jnp.dot((xf * xf).astype(BF16), sel, preferred_element_type=F32)
        return jnp.max(n2, axis=0, keepdims=True)

    @pl.when(i == 0)
    def _():
        def body(jj, best):
            kk = k_ref[pl.ds(pl.multiple_of(jj * t, t), t), :]
            return jnp.maximum(best, max_sq_norm(kk))
        kn2 = lax.fori_loop(0, k_ref.shape[0] // t, body, jnp.zeros((1, LANES), F32))
        kn_scr[...] = jnp.broadcast_to(kn2, kn_scr.shape)
        r = lax.broadcasted_iota(jnp.int32, (t, t), 0)
        c = lax.broadcasted_iota(jnp.int32, (t, t), 1)
        rel = (r - c).astype(F32) * slope
        rel_scr[...] = rel
        diag_scr[...] = jnp.where(c <= r, rel, jnp.inf)

    bound = jnp.sqrt(max_sq_norm(q_ref[...]) * kn_scr[0:1, :]) * NORM_BOUND_MARGIN + NORM_BOUND_SLACK

    def slack():
        gaps = [bound[:, mp:mp + 1] - jnp.min(m_scr[mp], axis=0, keepdims=True)[:, 0:1]
                for mp in range(len(maps))]
        return functools.reduce(jnp.maximum, gaps)[0, 0]

    diagonal_chunk()

    def live(j, slk):
        nearest = ((i - j - 1) * t + 1).astype(F32)
        return jnp.logical_and(j >= 0, slk - slope * nearest >= DIFF_DEAD_LOG2)

    def cond(st):
        return live(*st)

    def keep_two(j, slk):
        chunks_keep_max((j, j - 1))
        return j - 2, slk

    def keep_one(j, slk):
        chunks_keep_max((j,))
        return j - 1, slk

    def keep_max(j, slk):
        return lax.cond(live(j - 1, slk), keep_two, keep_one, j, slk)

    def update_max(j, slk):
        update_chunk(j)
        return j - 1, slack()

    def body(st):
        j, slk = st
        return lax.cond(slk <= DIFF_EXP_HEADROOM, keep_max, update_max, j, slk)

    lax.while_loop(cond, body, (i - 1, slack()))

    lam = lam_ref[0:1, :]
    l0 = jnp.sum(l_scr[0], axis=-1, keepdims=True)
    l1 = jnp.sum(l_scr[1], axis=-1, keepdims=True)
    o = a_scr[0] / l0 - lam * (a_scr[1] / l1)
    o_ref[...] = (_rms(o, subln_ref[...]) * out_scale).astype(o_ref.dtype)


def _diff_attn(proj, lam, subln, lambda_init):
    bsz, s, _ = proj.shape
    t = DIFF_T
    nh = DIFF_HEADS
    slopes = tuple(2.0 ** (-8.0 * (h + 1) / nh) for h in range(nh))
    return pl.pallas_call(
        functools.partial(_diff_attn_kernel, t=t, slopes=slopes, out_scale=1.0 - lambda_init),
        grid=(bsz, nh, s // t),
        in_specs=[
            pl.BlockSpec((None, t, LANES), lambda b, h, i: (b, i, h)),
            pl.BlockSpec((None, s, LANES), lambda b, h, i: (b, 0, nh + h)),
            pl.BlockSpec((None, s, LANES), lambda b, h, i: (b, 0, 2 * nh + h)),
            pl.BlockSpec((8, LANES), lambda b, h, i: (0, 0)),
            pl.BlockSpec((1, LANES), lambda b, h, i: (0, 0)),
        ],
        out_specs=pl.BlockSpec((None, t, LANES), lambda b, h, i: (b, i, h)),
        out_shape=jax.ShapeDtypeStruct((bsz, s, nh * LANES), BF16),
        scratch_shapes=[pltpu.VMEM((8, LANES), F32)] + [pltpu.VMEM((t, t), F32)] * 2
                       + [pltpu.VMEM((2, t, LANES), F32)] * 3,
        compiler_params=_cparams(("arbitrary", "arbitrary", "arbitrary")),
        name="diff_attn",
    )(proj, proj, proj, lam, subln.reshape(1, LANES))


def _sb_attn_kernel(q_ref, k_ref, v_ref, o_ref, *, t):
    i = pl.program_id(2)
    pair_cols = [slice(p * LANES, (p + 1) * LANES) for p in range(q_ref.shape[1] // LANES)]
    queries = [_split_heads(q_ref[:, cols]) for cols in pair_cols]
    n_heads = 2 * len(pair_cols)
    r = lax.broadcasted_iota(jnp.int32, (t, t), 0)
    c = lax.broadcasted_iota(jnp.int32, (t, t), 1)
    strict = c < r
    after = (r > c).astype(BF16)

    def prefix(j, *, diagonal, valid=None):
        start = pl.multiple_of(j * t, t)
        parts, values = [], []
        for cols, pair in zip(pair_cols, queries):
            k = k_ref[pl.ds(start, t), cols]
            values.extend(_split_heads(v_ref[pl.ds(start, t), cols]))
            for qm in pair:
                z = _dot_nt(qm, k)
                soft = jnp.log2(1.0 + jnp.exp2(-jnp.abs(z)))
                log_beta = jnp.minimum(z, 0.0) - soft
                x = log_beta - z
                if diagonal:
                    x = jnp.where(strict, x, 0.0)
                if valid is not None:
                    x = jnp.where(valid, x, 0.0)
                xh = x.astype(BF16)
                xl = (x - xh.astype(F32)).astype(BF16)
                inner = (jnp.dot(xh, after, preferred_element_type=F32)
                         + jnp.dot(xl, after, preferred_element_type=F32))
                parts.append((log_beta, inner, jnp.sum(x, axis=-1, keepdims=True)))
        return parts, values

    def finish(parts, values, carries, *, diagonal, valid=None):
        outs, new_carries = [], []
        for (log_beta, inner, total), vm, carry in zip(parts, values, carries):
            att = jnp.exp2(log_beta + (inner + carry))
            if diagonal:
                att = jnp.where(strict, att, 0.0)
            if valid is not None:
                att = jnp.where(valid, att, 0.0)
            outs.append(jnp.dot(att.astype(BF16), vm, preferred_element_type=F32))
            new_carries.append(carry + total)
        return new_carries, [outs[h] + outs[h + 1] for h in range(0, n_heads, 2)]

    has_prev = i > 0
    zero_col = jnp.zeros((t, 1), F32)
    diag_parts, diag_values = prefix(i, diagonal=True)
    prev_parts, prev_values = prefix(jnp.maximum(i - 1, 0), diagonal=False, valid=has_prev)
    carries, accs = finish(diag_parts, diag_values, [zero_col] * n_heads, diagonal=True)
    carries, accs_prev = finish(prev_parts, prev_values, carries, diagonal=False, valid=has_prev)
    accs = [a + b for a, b in zip(accs, accs_prev)]

    def alive(carries):
        return jnp.max(functools.reduce(jnp.maximum, carries)) > SB_DEAD_LOG2

    def cond(state):
        return jnp.logical_and(state[0] >= 0, state[-1])

    def body(state):
        j, carries, accs, _ = state
        parts, values = prefix(j, diagonal=False)
        carries, outs = finish(parts, values, carries, diagonal=False)
        return j - 1, carries, [a + b for a, b in zip(accs, outs)], alive(carries)

    _, _, accs, _ = lax.while_loop(cond, body, (i - 2, carries, accs, alive(carries)))
    for cols, acc in zip(pair_cols, accs):
        o_ref[:, cols] = acc


def _sb_attn(proj):
    bsz, s, _ = proj.shape
    t = SB_T
    pairs = SB_HEADS * HEAD_DIM // LANES
    width = SB_PAIRS_PER_STEP * LANES
    groups = pairs // SB_PAIRS_PER_STEP
    base = 3 * DIFF_HEADS // SB_PAIRS_PER_STEP
    whole_seq = lambda index_map: pl.BlockSpec((None, s, width), index_map, pipeline_mode=pl.Buffered(1))
    return pl.pallas_call(
        functools.partial(_sb_attn_kernel, t=t),
        grid=(bsz, groups, s // t),
        in_specs=[
            pl.BlockSpec((None, t, width), lambda b, g, i: (b, i, base + g)),
            whole_seq(lambda b, g, i: (b, 0, base + groups + g)),
            whole_seq(lambda b, g, i: (b, 0, base + 2 * groups + g)),
        ],
        out_specs=pl.BlockSpec((None, t, width), lambda b, g, i: (b, i, g)),
        out_shape=jax.ShapeDtypeStruct((bsz, s, pairs * LANES), F32),
        compiler_params=_cparams(("parallel", "parallel", "arbitrary")),
        name="sb_attn",
    )(proj, proj, proj)


def _mix_out_ffn_kernel(x_ref, od_ref, os_ref, mgt_ref, beta_ref, mgpost_ref, wod_ref, wos_ref,
                        sh_ref, sc_ref, gt_ref, gpre_ref, gpost_ref, wg_ref, wu_ref, wd_ref,
                        o_ref, x_scr, h_scr, a_scr, *, tf):
    osn = _rms(os_ref[...], beta_ref[...]).astype(BF16)
    y = (jnp.dot(od_ref[...], wod_ref[...], preferred_element_type=F32)
         + jnp.dot(osn, wos_ref[...], preferred_element_type=F32))
    x_scr[...] = x_ref[...] + MIX_RES_WEIGHT * mgt_ref[...] * _rms(y, mgpost_ref[...])
    o_ref[...] = _swiglu_sublayer(x_scr, sh_ref[...], sc_ref[...], gt_ref[...], gpre_ref[...], gpost_ref[...],
                                  wg_ref, wu_ref, wd_ref, h_scr, a_scr, tf)


def _mix_out_ffn(x, o_d, o_s, mod4, k_gate, sb_beta, mix_g_post, w_out, k0, g_pre, g_post, w_gate, w_up, w_down):
    bsz, s, d = x.shape
    dd, ds = o_d.shape[-1], o_s.shape[-1]
    d_ff = w_gate.shape[1]
    tm, tf = FFN_TM, FFN_TF
    w_bf = w_out.astype(BF16)
    tile = lambda n: pl.BlockSpec((None, tm, n), lambda b, i: (b, i, 0))
    mod_spec = lambda k: pl.BlockSpec((None, None, 1, d), lambda b, i: (b, k, 0, 0))
    whole = lambda r, c: _resident((r, c), lambda b, i: (0, 0))
    return pl.pallas_call(
        functools.partial(_mix_out_ffn_kernel, tf=tf),
        grid=(bsz, s // tm),
        in_specs=[
            tile(d), tile(dd), tile(ds),
            mod_spec(k_gate), whole(1, ds), whole(1, d), whole(dd, d), whole(ds, d),
            mod_spec(k0), mod_spec(k0 + 1), mod_spec(k0 + 2), whole(1, d), whole(1, d),
            whole(d, d_ff), whole(d, d_ff), whole(d_ff, d),
        ],
        out_specs=tile(d),
        out_shape=jax.ShapeDtypeStruct(x.shape, F32),
        scratch_shapes=[pltpu.VMEM((tm, d), F32), pltpu.VMEM((tm, d), BF16), pltpu.VMEM((tm, d_ff), BF16)],
        compiler_params=_cparams(("parallel", "parallel")),
        name="mix_out_ffn",
    )(x, o_d, o_s, mod4, sb_beta.reshape(1, ds), mix_g_post.reshape(1, d), w_bf[:dd], w_bf[dd:],
      mod4, mod4, mod4, g_pre.reshape(1, d), g_post.reshape(1, d),
      w_gate.astype(BF16), w_up.astype(BF16), w_down.astype(BF16))


def kernel(x, c, w_ada, b_ada, ffn1_g_pre, ffn1_g_post, ffn1_w_gate, ffn1_w_up, ffn1_w_down, mix_g_pre, mix_g_post, w_in, w_out, lam_q1, lam_k1, lam_q2, lam_k2, diff_subln, sb_beta, ffn2_g_pre, ffn2_g_post, ffn2_w_gate, ffn2_w_up, ffn2_w_down):
    bsz, _, d = x.shape
    depth = w_ada.shape[0]
    for l in range(depth):
        lambda_init = 0.8 - 0.6 * math.exp(-0.3 * l)
        lam_params = jnp.stack([lam_q1[l], lam_k1[l], lam_q2[l], lam_k2[l]])
        mod, lam = _adaln(c, w_ada[l], b_ada[l], lam_params, lambda_init)
        mod4 = mod.reshape(bsz, N_MOD, 1, d)
        x = _ffn(x, mod4, 0, ffn1_g_pre[l], ffn1_g_post[l], ffn1_w_gate[l], ffn1_w_up[l], ffn1_w_down[l])
        proj = _mix_in(x, mod4, 3, mix_g_pre[l], w_in[l])
        o_d = _diff_attn(proj, lam, diff_subln[l], lambda_init)
        o_s = _sb_attn(proj)
        x = _mix_out_ffn(x, o_d, o_s, mod4, 5, sb_beta[l], mix_g_post[l], w_out[l],
                         6, ffn2_g_pre[l], ffn2_g_post[l], ffn2_w_gate[l], ffn2_w_up[l], ffn2_w_down[l])
    return x
```

```python
import functools
import math

import jax
import jax.numpy as jnp
from jax import lax
from jax.experimental import pallas as pl
from jax.experimental.pallas import tpu as pltpu

F32 = jnp.float32
BF16 = jnp.bfloat16

HEAD_DIM = 64
DIFF_HEADS = 4
SB_HEADS = 8
N_MOD = 9
RMS_EPS = 1e-6
FFN_RES_WEIGHT = 0.5
MIX_RES_WEIGHT = 1.0

LANES = 128
VMEM_LIMIT_BYTES = 56 * 1024 * 1024

ADA_TN = 1024
FFN_TM = 512
FFN_TF = 256
DIFF_T = 512
SB_T = 256
SB_PAIRS_PER_STEP = 4

LOG2E = math.log2(math.e)
DIFF_Q_SCALE = LOG2E * HEAD_DIM ** -0.5
SB_Q_SCALE = LOG2E * HEAD_DIM ** -0.5

SB_DEAD_LOG2 = -160.0
DIFF_DEAD_LOG2 = -150.0
DIFF_EXP_HEADROOM = 100.0
NORM_BOUND_MARGIN = 1.01
NORM_BOUND_SLACK = 1.0


def _cparams(semantics):
    return pltpu.CompilerParams(dimension_semantics=semantics, vmem_limit_bytes=VMEM_LIMIT_BYTES)


def _resident(block_shape, index_map):
    return pl.BlockSpec(block_shape, index_map, pipeline_mode=pl.Buffered(1))


def _sigmoid(x):
    return 1.0 / (1.0 + jnp.exp(-x))


def _rms(x, gain):
    ms = jnp.mean(x * x, axis=-1, keepdims=True)
    return x * lax.rsqrt(ms + RMS_EPS) * gain


def _adaln_kernel(c_ref, w_ref, b_ref, lamp_ref, mod_ref, lam_ref, *, lambda_init):
    c = c_ref[...]
    s = c * _sigmoid(c)
    mod_ref[...] = jnp.dot(s, w_ref[...], preferred_element_type=F32,
                           precision=lax.Precision.HIGHEST) + b_ref[...]
    lp = lamp_ref[...]
    d1 = jnp.sum(lp[0:1] * lp[1:2], axis=-1, keepdims=True)
    d2 = jnp.sum(lp[2:3] * lp[3:4], axis=-1, keepdims=True)
    lam = jnp.exp(d1) - jnp.exp(d2) + lambda_init
    lam_ref[...] = jnp.broadcast_to(lam, lam_ref.shape)


def _adaln(c, w_ada, b_ada, lam_params, lambda_init):
    bsz, d = c.shape
    n = w_ada.shape[1]
    tn = ADA_TN
    return pl.pallas_call(
        functools.partial(_adaln_kernel, lambda_init=lambda_init),
        grid=(n // tn,),
        in_specs=[
            pl.BlockSpec((bsz, d), lambda j: (0, 0)),
            pl.BlockSpec((d, tn), lambda j: (0, j)),
            pl.BlockSpec((1, tn), lambda j: (0, j)),
            pl.BlockSpec(lam_params.shape, lambda j: (0, 0)),
        ],
        out_specs=[
            pl.BlockSpec((bsz, tn), lambda j: (0, j)),
            pl.BlockSpec((8, LANES), lambda j: (0, 0)),
        ],
        out_shape=[
            jax.ShapeDtypeStruct((bsz, n), F32),
            jax.ShapeDtypeStruct((8, LANES), F32),
        ],
        compiler_params=_cparams(("arbitrary",)),
        name="adaln",
    )(c, w_ada, b_ada.reshape(1, n), lam_params)


def _modulated_norm(x, g_pre, scale, shift):
    return _rms(x, g_pre) * (1.0 + scale) + shift


def _swiglu_sublayer(x_ref, sh, sc, gt, g_pre, g_post, wg_ref, wu_ref, wd_ref, h_scr, a_scr, tf):
    h_scr[...] = _modulated_norm(x_ref[...], g_pre, sc, sh).astype(BF16)
    d_ff = wg_ref.shape[1]
    for j in range(d_ff // tf):
        cols = slice(j * tf, (j + 1) * tf)
        h = h_scr[...]
        g = jnp.dot(h, wg_ref[:, cols], preferred_element_type=F32)
        u = jnp.dot(h, wu_ref[:, cols], preferred_element_type=F32)
        a_scr[:, cols] = (g * _sigmoid(g) * u).astype(BF16)
    y = jnp.dot(a_scr[...], wd_ref[...], preferred_element_type=F32)
    return x_ref[...] + FFN_RES_WEIGHT * gt * _rms(y, g_post)


def _ffn_kernel(x_ref, sh_ref, sc_ref, gt_ref, gpre_ref, gpost_ref, wg_ref, wu_ref, wd_ref,
                o_ref, h_scr, a_scr, *, tf):
    o_ref[...] = _swiglu_sublayer(x_ref, sh_ref[...], sc_ref[...], gt_ref[...], gpre_ref[...], gpost_ref[...],
                                  wg_ref, wu_ref, wd_ref, h_scr, a_scr, tf)


def _ffn(x, mod4, k0, g_pre, g_post, w_gate, w_up, w_down):
    bsz, s, d = x.shape
    d_ff = w_gate.shape[1]
    tm, tf = FFN_TM, FFN_TF
    mod_spec = lambda k: pl.BlockSpec((None, None, 1, d), lambda b, i: (b, k, 0, 0))
    row = lambda: _resident((1, d), lambda b, i: (0, 0))
    return pl.pallas_call(
        functools.partial(_ffn_kernel, tf=tf),
        grid=(bsz, s // tm),
        in_specs=[
            pl.BlockSpec((None, tm, d), lambda b, i: (b, i, 0)),
            mod_spec(k0), mod_spec(k0 + 1), mod_spec(k0 + 2),
            row(), row(),
            _resident((d, d_ff), lambda b, i: (0, 0)),
            _resident((d, d_ff), lambda b, i: (0, 0)),
            _resident((d_ff, d), lambda b, i: (0, 0)),
        ],
        out_specs=pl.BlockSpec((None, tm, d), lambda b, i: (b, i, 0)),
        out_shape=jax.ShapeDtypeStruct(x.shape, F32),
        scratch_shapes=[pltpu.VMEM((tm, d), BF16), pltpu.VMEM((tm, d_ff), BF16)],
        compiler_params=_cparams(("parallel", "parallel")),
        name="ffn",
    )(x, mod4, mod4, mod4, g_pre.reshape(1, d), g_post.reshape(1, d),
      w_gate.astype(BF16), w_up.astype(BF16), w_down.astype(BF16))


def _mix_in_kernel(x_ref, sh_ref, sc_ref, gpre_ref, w_ref, o_ref, h_scr, *, group, group_scale):
    h_scr[...] = _modulated_norm(x_ref[...], gpre_ref[...], sc_ref[...], sh_ref[...]).astype(BF16)
    for g in range(w_ref.shape[1] // group):
        cols = slice(g * group, (g + 1) * group)
        p = jnp.dot(h_scr[...], w_ref[:, cols], preferred_element_type=F32)
        if g in group_scale:
            p = p * group_scale[g]
        o_ref[:, cols] = p.astype(BF16)


def _mix_in(x, mod4, k0, g_pre, w_in):
    bsz, s, d = x.shape
    n = w_in.shape[1]
    group = n // 6
    tm = FFN_TM
    mod_spec = lambda k: pl.BlockSpec((None, None, 1, d), lambda b, i: (b, k, 0, 0))
    return pl.pallas_call(
        functools.partial(_mix_in_kernel, group=group,
                          group_scale={0: DIFF_Q_SCALE, 3: SB_Q_SCALE}),
        grid=(bsz, s // tm),
        in_specs=[
            pl.BlockSpec((None, tm, d), lambda b, i: (b, i, 0)),
            mod_spec(k0), mod_spec(k0 + 1),
            _resident((1, d), lambda b, i: (0, 0)),
            _resident((d, n), lambda b, i: (0, 0)),
        ],
        out_specs=pl.BlockSpec((None, tm, n), lambda b, i: (b, i, 0)),
        out_shape=jax.ShapeDtypeStruct((bsz, s, n), BF16),
        scratch_shapes=[pltpu.VMEM((tm, d), BF16)],
        compiler_params=_cparams(("parallel", "parallel")),
        name="mix_in",
    )(x, mod4, mod4, g_pre.reshape(1, d), w_in.astype(BF16))


def _dot_nt(a, b):
    return lax.dot_general(a, b, (((1,), (1,)), ((), ())), preferred_element_type=F32)


def _split_heads(x):
    lane = lax.broadcasted_iota(jnp.int32, x.shape, 1)
    zero = jnp.zeros_like(x)
    return jnp.where(lane < HEAD_DIM, x, zero), jnp.where(lane >= HEAD_DIM, x, zero)


def _diff_attn_kernel(q_ref, k_ref, v_ref, lam_ref, subln_ref, o_ref, kn_scr, rel_scr, diag_scr,
                      m_scr, l_scr, a_scr, *, t, slopes, out_scale):
    head = pl.program_id(1)
    i = pl.program_id(2)
    slope = jnp.float32(slopes[-1])
    for hh in range(len(slopes) - 2, -1, -1):
        slope = jnp.where(head == hh, jnp.float32(slopes[hh]), slope)
    slope = slope * LOG2E

    maps = _split_heads(q_ref[...])
    lane_tiles = t // LANES

    def lane_partial_sum(p):
        parts = [p[:, n * LANES:(n + 1) * LANES] for n in range(p.shape[1] // LANES)]
        return functools.reduce(lambda a, b: a + b, parts)

    def load_chunk(j):
        start = pl.multiple_of(j * t, t)
        off = slope * ((i - j) * t).astype(F32)
        return k_ref[pl.ds(start, t), :], v_ref[pl.ds(start, t), :], off

    def diagonal_chunk():
        k, v, _ = load_chunk(i)
        for mp, qm in enumerate(maps):
            s = _dot_nt(qm, k) - diag_scr[...]
            m_new = jnp.broadcast_to(jnp.max(s, axis=-1, keepdims=True), (t, LANES))
            p = jnp.exp2(s - jnp.tile(m_new, (1, lane_tiles)))
            m_scr[mp] = m_new
            l_scr[mp] = lane_partial_sum(p)
            a_scr[mp] = jnp.dot(p.astype(BF16), v, preferred_element_type=F32)

    def update_chunk(j):
        k, v, off = load_chunk(j)
        for mp, qm in enumerate(maps):
            s = _dot_nt(qm, k) - rel_scr[...]
            m = m_scr[mp]
            m_new = jnp.maximum(m, jnp.max(s, axis=-1, keepdims=True) - off)
            alpha = jnp.exp2(m - m_new)
            p = jnp.exp2(s - jnp.tile(m_new + off, (1, lane_tiles)))
            m_scr[mp] = m_new
            l_scr[mp] = alpha * l_scr[mp] + lane_partial_sum(p)
            a_scr[mp] = alpha * a_scr[mp] + jnp.dot(p.astype(BF16), v, preferred_element_type=F32)

    def chunks_keep_max(js):
        loaded = [load_chunk(j) for j in js]
        for mp, qm in enumerate(maps):
            m = m_scr[mp]
            l_new = a_new = None
            for k, v, off in loaded:
                p = jnp.exp2(_dot_nt(qm, k) - rel_scr[...] - jnp.tile(m + off, (1, lane_tiles)))
                ls = lane_partial_sum(p)
                av = jnp.dot(p.astype(BF16), v, preferred_element_type=F32)
                l_new = ls if l_new is None else l_new + ls
                a_new = av if a_new is None else a_new + av
            l_scr[mp] = l_scr[mp] + l_new
            a_scr[mp] = a_scr[mp] + a_new

    sel_r = lax.broadcasted_iota(jnp.int32, (LANES, LANES), 0)
    sel_c = lax.broadcasted_iota(jnp.int32, (LANES, LANES), 1)
    sel = jnp.where(sel_c == lax.shift_right_logical(sel_r, HEAD_DIM.bit_length() - 1), 1.0, 0.0).astype(BF16)

    def max_sq_norm(x):
        xf = x.astype(F32)
        n2 = jnp.dot((xf * xf).astype(BF16), sel, preferred_element_type=F32)
        return jnp.max(n2, axis=0, keepdims=True)

    @pl.when(i == 0)
    def _():
        def body(jj, best):
            kk = k_ref[pl.ds(pl.multiple_of(jj * t, t), t), :]
            return jnp.maximum(best, max_sq_norm(kk))
        kn2 = lax.fori_loop(0, k_ref.shape[0] // t, body, jnp.zeros((1, LANES), F32))
        kn_scr[...] = jnp.broadcast_to(kn2, kn_scr.shape)
        r = lax.broadcasted_iota(jnp.int32, (t, t), 0)
        c = lax.broadcasted_iota(jnp.int32, (t, t), 1)
        rel = (r - c).astype(F32) * slope
        rel_scr[...] = rel
        diag_scr[...] = jnp.where(c <= r, rel, jnp.inf)

    bound = jnp.sqrt(max_sq_norm(q_ref[...]) * kn_scr[0:1, :]) * NORM_BOUND_MARGIN + NORM_BOUND_SLACK

    def slack():
        gaps = [bound[:, mp:mp + 1] - jnp.min(m_scr[mp], axis=0, keepdims=True)[:, 0:1]
                for mp in range(len(maps))]
        return functools.reduce(jnp.maximum, gaps)[0, 0]

    diagonal_chunk()

    def live(j, slk):
        nearest = ((i - j - 1) * t + 1).astype(F32)
        return jnp.logical_and(j >= 0, slk - slope * nearest >= DIFF_DEAD_LOG2)

    def cond(st):
        return live(*st)

    def keep_two(j, slk):
        chunks_keep_max((j, j - 1))
        return j - 2, slk

    def keep_one(j, slk):
        chunks_keep_max((j,))
        return j - 1, slk

    def keep_four(j, slk):
        chunks_keep_max((j, j - 1, j - 2, j - 3))
        return j - 4, slk

    def keep_few(j, slk):
        return lax.cond(live(j - 1, slk), keep_two, keep_one, j, slk)

    def keep_max(j, slk):
        return lax.cond(live(j - 3, slk), keep_four, keep_few, j, slk)

    def update_max(j, slk):
        update_chunk(j)
        return j - 1, slack()

    def body(st):
        j, slk = st
        return lax.cond(slk <= DIFF_EXP_HEADROOM, keep_max, update_max, j, slk)

    lax.while_loop(cond, body, (i - 1, slack()))

    lam = lam_ref[0:1, :]
    l0 = jnp.sum(l_scr[0], axis=-1, keepdims=True)
    l1 = jnp.sum(l_scr[1], axis=-1, keepdims=True)
    o = a_scr[0] / l0 - lam * (a_scr[1] / l1)
    o_ref[...] = (_rms(o, subln_ref[...]) * out_scale).astype(o_ref.dtype)


def _diff_attn(proj, lam, subln, lambda_init):
    bsz, s, _ = proj.shape
    t = DIFF_T
    nh = DIFF_HEADS
    slopes = tuple(2.0 ** (-8.0 * (h + 1) / nh) for h in range(nh))
    return pl.pallas_call(
        functools.partial(_diff_attn_kernel, t=t, slopes=slopes, out_scale=1.0 - lambda_init),
        grid=(bsz, nh, s // t),
        in_specs=[
            pl.BlockSpec((None, t, LANES), lambda b, h, i: (b, i, h)),
            pl.BlockSpec((None, s, LANES), lambda b, h, i: (b, 0, nh + h)),
            pl.BlockSpec((None, s, LANES), lambda b, h, i: (b, 0, 2 * nh + h)),
            pl.BlockSpec((8, LANES), lambda b, h, i: (0, 0)),
            pl.BlockSpec((1, LANES), lambda b, h, i: (0, 0)),
        ],
        out_specs=pl.BlockSpec((None, t, LANES), lambda b, h, i: (b, i, h)),
        out_shape=jax.ShapeDtypeStruct((bsz, s, nh * LANES), BF16),
        scratch_shapes=[pltpu.VMEM((8, LANES), F32)] + [pltpu.VMEM((t, t), F32)] * 2
                       + [pltpu.VMEM((2, t, LANES), F32)] * 3,
        compiler_params=_cparams(("arbitrary", "arbitrary", "arbitrary")),
        name="diff_attn",
    )(proj, proj, proj, lam, subln.reshape(1, LANES))


def _sb_attn_kernel(q_ref, k_ref, v_ref, o_ref, *, t):
    i = pl.program_id(2)
    pair_cols = [slice(p * LANES, (p + 1) * LANES) for p in range(q_ref.shape[1] // LANES)]
    queries = [_split_heads(q_ref[:, cols]) for cols in pair_cols]
    n_heads = 2 * len(pair_cols)
    r = lax.broadcasted_iota(jnp.int32, (t, t), 0)
    c = lax.broadcasted_iota(jnp.int32, (t, t), 1)
    strict = c < r
    after = (r > c).astype(BF16)
    after2 = jnp.concatenate([after, after], axis=0)

    def prefix(j, *, diagonal, valid=None):
        start = pl.multiple_of(j * t, t)
        parts, values = [], []
        for cols, pair in zip(pair_cols, queries):
            k = k_ref[pl.ds(start, t), cols]
            v = v_ref[pl.ds(start, t), cols]
            if valid is not None:
                v = jnp.where(valid, v, jnp.zeros_like(v))
            values.extend(_split_heads(v))
            for qm in pair:
                z = _dot_nt(qm, k)
                soft = jnp.log2(1.0 + jnp.exp2(-jnp.abs(z)))
                log_beta = jnp.minimum(z, 0.0) - soft
                x = log_beta - z
                if diagonal:
                    x = jnp.where(strict, x, 0.0)
                xh = x.astype(BF16)
                xl = (x - xh.astype(F32)).astype(BF16)
                inner = jnp.dot(jnp.concatenate([xh, xl], axis=1), after2, preferred_element_type=F32)
                total = inner[:, 0:1] + x[:, 0:1]
                if valid is not None:
                    total = jnp.where(valid, total, 0.0)
                parts.append((log_beta, inner, total))
        return parts, values

    def finish(parts, values, carries, *, diagonal):
        outs, new_carries = [], []
        for (log_beta, inner, total), vm, carry in zip(parts, values, carries):
            att = jnp.exp2(log_beta + (inner + carry))
            if diagonal:
                att = jnp.where(strict, att, 0.0)
            outs.append(jnp.dot(att.astype(BF16), vm, preferred_element_type=F32))
            new_carries.append(carry + total)
        return new_carries, [outs[h] + outs[h + 1] for h in range(0, n_heads, 2)]

    has_prev = i > 0
    zero_col = jnp.zeros((t, 1), F32)
    diag_parts, diag_values = prefix(i, diagonal=True)
    prev_parts, prev_values = prefix(jnp.maximum(i - 1, 0), diagonal=False, valid=has_prev)
    carries, accs = finish(diag_parts, diag_values, [zero_col] * n_heads, diagonal=True)
    carries, accs_prev = finish(prev_parts, prev_values, carries, diagonal=False)
    accs = [a + b for a, b in zip(accs, accs_prev)]

    def alive(carries):
        return jnp.max(functools.reduce(jnp.maximum, carries)) > SB_DEAD_LOG2

    def cond(state):
        return jnp.logical_and(state[0] >= 0, state[-1])

    def body(state):
        j, carries, accs, _ = state
        parts, values = prefix(j, diagonal=False)
        carries, outs = finish(parts, values, carries, diagonal=False)
        return j - 1, carries, [a + b for a, b in zip(accs, outs)], alive(carries)

    _, _, accs, _ = lax.while_loop(cond, body, (i - 2, carries, accs, alive(carries)))
    for cols, acc in zip(pair_cols, accs):
        o_ref[:, cols] = acc


def _sb_attn(proj):
    bsz, s, _ = proj.shape
    t = SB_T
    pairs = SB_HEADS * HEAD_DIM // LANES
    width = SB_PAIRS_PER_STEP * LANES
    groups = pairs // SB_PAIRS_PER_STEP
    base = 3 * DIFF_HEADS // SB_PAIRS_PER_STEP
    whole_seq = lambda index_map: pl.BlockSpec((None, s, width), index_map, pipeline_mode=pl.Buffered(1))
    return pl.pallas_call(
        functools.partial(_sb_attn_kernel, t=t),
        grid=(bsz, groups, s // t),
        in_specs=[
            pl.BlockSpec((None, t, width), lambda b, g, i: (b, i, base + g)),
            whole_seq(lambda b, g, i: (b, 0, base + groups + g)),
            whole_seq(lambda b, g, i: (b, 0, base + 2 * groups + g)),
        ],
        out_specs=pl.BlockSpec((None, t, width), lambda b, g, i: (b, i, g)),
        out_shape=jax.ShapeDtypeStruct((bsz, s, pairs * LANES), F32),
        compiler_params=_cparams(("parallel", "parallel", "arbitrary")),
        name="sb_attn",
    )(proj, proj, proj)


def _mix_out_ffn_kernel(x_ref, od_ref, os_ref, mgt_ref, beta_ref, mgpost_ref, wod_ref, wos_ref,
                        sh_ref, sc_ref, gt_ref, gpre_ref, gpost_ref, wg_ref, wu_ref, wd_ref,
                        o_ref, x_scr, h_scr, a_scr, *, tf):
    osn = _rms(os_ref[...], beta_ref[...]).astype(BF16)
    y = (jnp.dot(od_ref[...], wod_ref[...], preferred_element_type=F32)
         + jnp.dot(osn, wos_ref[...], preferred_element_type=F32))
    x_scr[...] = x_ref[...] + MIX_RES_WEIGHT * mgt_ref[...] * _rms(y, mgpost_ref[...])
    o_ref[...] = _swiglu_sublayer(x_scr, sh_ref[...], sc_ref[...], gt_ref[...], gpre_ref[...], gpost_ref[...],
                                  wg_ref, wu_ref, wd_ref, h_scr, a_scr, tf)


def _mix_out_ffn(x, o_d, o_s, mod4, k_gate, sb_beta, mix_g_post, w_out, k0, g_pre, g_post, w_gate, w_up, w_down):
    bsz, s, d = x.shape
    dd, ds = o_d.shape[-1], o_s.shape[-1]
    d_ff = w_gate.shape[1]
    tm, tf = FFN_TM, FFN_TF
    w_bf = w_out.astype(BF16)
    tile = lambda n: pl.BlockSpec((None, tm, n), lambda b, i: (b, i, 0))
    mod_spec = lambda k: pl.BlockSpec((None, None, 1, d), lambda b, i: (b, k, 0, 0))
    whole = lambda r, c: _resident((r, c), lambda b, i: (0, 0))
    return pl.pallas_call(
        functools.partial(_mix_out_ffn_kernel, tf=tf),
        grid=(bsz, s // tm),
        in_specs=[
            tile(d), tile(dd), tile(ds),
            mod_spec(k_gate), whole(1, ds), whole(1, d), whole(dd, d), whole(ds, d),
            mod_spec(k0), mod_spec(k0 + 1), mod_spec(k0 + 2), whole(1, d), whole(1, d),
            whole(d, d_ff), whole(d, d_ff), whole(d_ff, d),
        ],
        out_specs=tile(d),
        out_shape=jax.ShapeDtypeStruct(x.shape, F32),
        scratch_shapes=[pltpu.VMEM((tm, d), F32), pltpu.VMEM((tm, d), BF16), pltpu.VMEM((tm, d_ff), BF16)],
        compiler_params=_cparams(("parallel", "parallel")),
        name="mix_out_ffn",
    )(x, o_d, o_s, mod4, sb_beta.reshape(1, ds), mix_g_post.reshape(1, d), w_bf[:dd], w_bf[dd:],
      mod4, mod4, mod4, g_pre.reshape(1, d), g_post.reshape(1, d),
      w_gate.astype(BF16), w_up.astype(BF16), w_down.astype(BF16))


def kernel(x, c, w_ada, b_ada, ffn1_g_pre, ffn1_g_post, ffn1_w_gate, ffn1_w_up, ffn1_w_down, mix_g_pre, mix_g_post, w_in, w_out, lam_q1, lam_k1, lam_q2, lam_k2, diff_subln, sb_beta, ffn2_g_pre, ffn2_g_post, ffn2_w_gate, ffn2_w_up, ffn2_w_down):
    bsz, _, d = x.shape
    depth = w_ada.shape[0]
    for l in range(depth):
        lambda_init = 0.8 - 0.6 * math.exp(-0.3 * l)
        lam_params = jnp.stack([lam_q1[l], lam_k1[l], lam_q2[l], lam_k2[l]])
        mod, lam = _adaln(c, w_ada[l], b_ada[l], lam_params, lambda_init)
        mod4 = mod.reshape(bsz, N_MOD, 1, d)
        x = _ffn(x, mod4, 0, ffn1_g_pre[l], ffn1_g_post[l], ffn1_w_gate[l], ffn1_w_up[l], ffn1_w_down[l])
        proj = _mix_in(x, mod4, 3, mix_g_pre[l], w_in[l])
        o_d = _diff_attn(proj, lam, diff_subln[l], lambda_init)
        o_s = _sb_attn(proj)
        x = _mix_out_ffn(x, o_d, o_s, mod4, 5, sb_beta[l], mix_g_post[l], w_out[l],
                         6, ffn2_g_pre[l], ffn2_g_post[l], ffn2_w_gate[l], ffn2_w_up[l], ffn2_w_down[l])
    return x
```

```python
import functools
import math

import jax
import jax.numpy as jnp
from jax import lax
from jax.experimental import pallas as pl
from jax.experimental.pallas import tpu as pltpu

F32 = jnp.float32
BF16 = jnp.bfloat16

HEAD_DIM = 64
DIFF_HEADS = 4
SB_HEADS = 8
N_MOD = 9
RMS_EPS = 1e-6
FFN_RES_WEIGHT = 0.5
MIX_RES_WEIGHT = 1.0

LANES = 128
VMEM_LIMIT_BYTES = 56 * 1024 * 1024

ADA_TN = 1024
FFN_TM = 512
FFN_TF = 256
DIFF_T = 512
SB_T = 256
SB_PAIRS_PER_STEP = 4

LOG2E = math.log2(math.e)
DIFF_Q_SCALE = LOG2E * HEAD_DIM ** -0.5
SB_Q_SCALE = LOG2E * HEAD_DIM ** -0.5

SB_DEAD_LOG2 = -160.0
DIFF_DEAD_LOG2 = -150.0
DIFF_EXP_HEADROOM = 100.0
NORM_BOUND_MARGIN = 1.01
NORM_BOUND_SLACK = 1.0


def _cparams(semantics):
    return pltpu.CompilerParams(dimension_semantics=semantics, vmem_limit_bytes=VMEM_LIMIT_BYTES)


def _resident(block_shape, index_map):
    return pl.BlockSpec(block_shape, index_map, pipeline_mode=pl.Buffered(1))


def _sigmoid(x):
    return 1.0 / (1.0 + jnp.exp(-x))


def _rms(x, gain):
    ms = jnp.mean(x * x, axis=-1, keepdims=True)
    return x * lax.rsqrt(ms + RMS_EPS) * gain


def _adaln_kernel(c_ref, w_ref, b_ref, lamp_ref, mod_ref, lam_ref, *, lambda_init):
    c = c_ref[...]
    s = c * _sigmoid(c)
    mod_ref[...] = jnp.dot(s, w_ref[...], preferred_element_type=F32,
                           precision=lax.Precision.HIGHEST) + b_ref[...]
    lp = lamp_ref[...]
    d1 = jnp.sum(lp[0:1] * lp[1:2], axis=-1, keepdims=True)
    d2 = jnp.sum(lp[2:3] * lp[3:4], axis=-1, keepdims=True)
    lam = jnp.exp(d1) - jnp.exp(d2) + lambda_init
    lam_ref[...] = jnp.broadcast_to(lam, lam_ref.shape)


def _adaln(c, w_ada, b_ada, lam_params, lambda_init):
    bsz, d = c.shape
    n = w_ada.shape[1]
    tn = ADA_TN
    return pl.pallas_call(
        functools.partial(_adaln_kernel, lambda_init=lambda_init),
        grid=(n // tn,),
        in_specs=[
            pl.BlockSpec((bsz, d), lambda j: (0, 0)),
            pl.BlockSpec((d, tn), lambda j: (0, j)),
            pl.BlockSpec((1, tn), lambda j: (0, j)),
            pl.BlockSpec(lam_params.shape, lambda j: (0, 0)),
        ],
        out_specs=[
            pl.BlockSpec((bsz, tn), lambda j: (0, j)),
            pl.BlockSpec((8, LANES), lambda j: (0, 0)),
        ],
        out_shape=[
            jax.ShapeDtypeStruct((bsz, n), F32),
            jax.ShapeDtypeStruct((8, LANES), F32),
        ],
        compiler_params=_cparams(("arbitrary",)),
        name="adaln",
    )(c, w_ada, b_ada.reshape(1, n), lam_params)


def _modulated_norm(x, g_pre, scale, shift):
    return _rms(x, g_pre) * (1.0 + scale) + shift


def _swiglu_sublayer(x_ref, sh, sc, gt, g_pre, g_post, wg_ref, wu_ref, wd_ref, h_scr, a_scr, y_scr, tf):
    x = x_ref[...]
    inv = lax.rsqrt(jnp.mean(x * x, axis=-1, keepdims=True) + RMS_EPS)
    h_scr[...] = (x * inv * (g_pre * (1.0 + sc)) + sh).astype(BF16)
    d_ff = wg_ref.shape[1]
    for j in range(d_ff // tf):
        cols = slice(j * tf, (j + 1) * tf)
        h = h_scr[...]
        g = jnp.dot(h, wg_ref[:, cols], preferred_element_type=F32)
        u = jnp.dot(h, wu_ref[:, cols], preferred_element_type=F32)
        a_scr[:, cols] = (g * _sigmoid(g) * u).astype(BF16)
    d = wd_ref.shape[1]
    sq = None
    for n in range(d // tf):
        cols = slice(n * tf, (n + 1) * tf)
        y = jnp.dot(a_scr[...], wd_ref[:, cols], preferred_element_type=F32)
        y_scr[:, cols] = y
        part = jnp.sum(y * y, axis=-1, keepdims=True)
        sq = part if sq is None else sq + part
    inv = lax.rsqrt(sq * (1.0 / d) + RMS_EPS)
    return x_ref[...] + y_scr[...] * inv * (FFN_RES_WEIGHT * gt * g_post)


def _ffn_kernel(x_ref, sh_ref, sc_ref, gt_ref, gpre_ref, gpost_ref, wg_ref, wu_ref, wd_ref,
                o_ref, h_scr, a_scr, y_scr, *, tf):
    o_ref[...] = _swiglu_sublayer(x_ref, sh_ref[...], sc_ref[...], gt_ref[...], gpre_ref[...], gpost_ref[...],
                                  wg_ref, wu_ref, wd_ref, h_scr, a_scr, y_scr, tf)


def _ffn(x, mod4, k0, g_pre, g_post, w_gate, w_up, w_down):
    bsz, s, d = x.shape
    d_ff = w_gate.shape[1]
    tm, tf = FFN_TM, FFN_TF
    mod_spec = lambda k: pl.BlockSpec((None, None, 1, d), lambda b, i: (b, k, 0, 0))
    row = lambda: _resident((1, d), lambda b, i: (0, 0))
    return pl.pallas_call(
        functools.partial(_ffn_kernel, tf=tf),
        grid=(bsz, s // tm),
        in_specs=[
            pl.BlockSpec((None, tm, d), lambda b, i: (b, i, 0)),
            mod_spec(k0), mod_spec(k0 + 1), mod_spec(k0 + 2),
            row(), row(),
            _resident((d, d_ff), lambda b, i: (0, 0)),
            _resident((d, d_ff), lambda b, i: (0, 0)),
            _resident((d_ff, d), lambda b, i: (0, 0)),
        ],
        out_specs=pl.BlockSpec((None, tm, d), lambda b, i: (b, i, 0)),
        out_shape=jax.ShapeDtypeStruct(x.shape, F32),
        scratch_shapes=[pltpu.VMEM((tm, d), BF16), pltpu.VMEM((tm, d_ff), BF16), pltpu.VMEM((tm, d), F32)],
        compiler_params=_cparams(("parallel", "parallel")),
        name="ffn",
    )(x, mod4, mod4, mod4, g_pre.reshape(1, d), g_post.reshape(1, d),
      w_gate.astype(BF16), w_up.astype(BF16), w_down.astype(BF16))


def _mix_in_kernel(x_ref, sh_ref, sc_ref, gpre_ref, w_ref, o_ref, h_scr, *, group, group_scale):
    h_scr[...] = _modulated_norm(x_ref[...], gpre_ref[...], sc_ref[...], sh_ref[...]).astype(BF16)
    for g in range(w_ref.shape[1] // group):
        cols = slice(g * group, (g + 1) * group)
        p = jnp.dot(h_scr[...], w_ref[:, cols], preferred_element_type=F32)
        if g in group_scale:
            p = p * group_scale[g]
        o_ref[:, cols] = p.astype(BF16)


def _mix_in(x, mod4, k0, g_pre, w_in):
    bsz, s, d = x.shape
    n = w_in.shape[1]
    group = n // 6
    tm = FFN_TM
    mod_spec = lambda k: pl.BlockSpec((None, None, 1, d), lambda b, i: (b, k, 0, 0))
    return pl.pallas_call(
        functools.partial(_mix_in_kernel, group=group,
                          group_scale={0: DIFF_Q_SCALE, 3: SB_Q_SCALE}),
        grid=(bsz, s // tm),
        in_specs=[
            pl.BlockSpec((None, tm, d), lambda b, i: (b, i, 0)),
            mod_spec(k0), mod_spec(k0 + 1),
            _resident((1, d), lambda b, i: (0, 0)),
            _resident((d, n), lambda b, i: (0, 0)),
        ],
        out_specs=pl.BlockSpec((None, tm, n), lambda b, i: (b, i, 0)),
        out_shape=jax.ShapeDtypeStruct((bsz, s, n), BF16),
        scratch_shapes=[pltpu.VMEM((tm, d), BF16)],
        compiler_params=_cparams(("parallel", "parallel")),
        name="mix_in",
    )(x, mod4, mod4, g_pre.reshape(1, d), w_in.astype(BF16))


def _dot_nt(a, b):
    return lax.dot_general(a, b, (((1,), (1,)), ((), ())), preferred_element_type=F32)


def _split_heads(x):
    lane = lax.broadcasted_iota(jnp.int32, x.shape, 1)
    zero = jnp.zeros_like(x)
    return jnp.where(lane < HEAD_DIM, x, zero), jnp.where(lane >= HEAD_DIM, x, zero)


def _diff_attn_kernel(q_ref, k_ref, v_ref, lam_ref, subln_ref, o_ref, kn_scr, rel_scr, diag_scr,
                      m_scr, l_scr, a_scr, *, t, slopes, out_scale):
    head = pl.program_id(1)
    i = pl.program_id(2)
    slope = jnp.float32(slopes[-1])
    for hh in range(len(slopes) - 2, -1, -1):
        slope = jnp.where(head == hh, jnp.float32(slopes[hh]), slope)
    slope = slope * LOG2E

    maps = _split_heads(q_ref[...])
    lane_tiles = t // LANES

    def lane_partial_sum(p):
        parts = [p[:, n * LANES:(n + 1) * LANES] for n in range(p.shape[1] // LANES)]
        return functools.reduce(lambda a, b: a + b, parts)

    def load_chunk(j):
        start = pl.multiple_of(j * t, t)
        off = slope * ((i - j) * t).astype(F32)
        return k_ref[pl.ds(start, t), :], v_ref[pl.ds(start, t), :], off

    def diagonal_chunk():
        k, v, _ = load_chunk(i)
        for mp, qm in enumerate(maps):
            s = _dot_nt(qm, k) - diag_scr[...]
            m_new = jnp.broadcast_to(jnp.max(s, axis=-1, keepdims=True), (t, LANES))
            p = jnp.exp2(s - jnp.tile(m_new, (1, lane_tiles)))
            m_scr[mp] = m_new
            l_scr[mp] = lane_partial_sum(p)
            a_scr[mp] = jnp.dot(p.astype(BF16), v, preferred_element_type=F32)

    def update_chunk(j):
        k, v, off = load_chunk(j)
        for mp, qm in enumerate(maps):
            s = _dot_nt(qm, k) - rel_scr[...]
            m = m_scr[mp]
            m_new = jnp.maximum(m, jnp.max(s, axis=-1, keepdims=True) - off)
            alpha = jnp.exp2(m - m_new)
            p = jnp.exp2(s - jnp.tile(m_new + off, (1, lane_tiles)))
            m_scr[mp] = m_new
            l_scr[mp] = alpha * l_scr[mp] + lane_partial_sum(p)
            a_scr[mp] = alpha * a_scr[mp] + jnp.dot(p.astype(BF16), v, preferred_element_type=F32)

    def chunks_keep_max(js):
        loaded = [load_chunk(j) for j in js]
        for mp, qm in enumerate(maps):
            m = m_scr[mp]
            l_new = a_new = None
            for k, v, off in loaded:
                p = jnp.exp2(_dot_nt(qm, k) - rel_scr[...] - jnp.tile(m + off, (1, lane_tiles)))
                ls = lane_partial_sum(p)
                av = jnp.dot(p.astype(BF16), v, preferred_element_type=F32)
                l_new = ls if l_new is None else l_new + ls
                a_new = av if a_new is None else a_new + av
            l_scr[mp] = l_scr[mp] + l_new
            a_scr[mp] = a_scr[mp] + a_new

    sel_r = lax.broadcasted_iota(jnp.int32, (LANES, LANES), 0)
    sel_c = lax.broadcasted_iota(jnp.int32, (LANES, LANES), 1)
    sel = jnp.where(sel_c == lax.shift_right_logical(sel_r, HEAD_DIM.bit_length() - 1), 1.0, 0.0).astype(BF16)

    def max_sq_norm(x):
        xf = x.astype(F32)
        n2 = jnp.dot((xf * xf).astype(BF16), sel, preferred_element_type=F32)
        return jnp.max(n2, axis=0, keepdims=True)

    @pl.when(i == 0)
    def _():
        def body(jj, best):
            kk = k_ref[pl.ds(pl.multiple_of(jj * t, t), t), :]
            return jnp.maximum(best, max_sq_norm(kk))
        kn2 = lax.fori_loop(0, k_ref.shape[0] // t, body, jnp.zeros((1, LANES), F32))
        kn_scr[...] = jnp.broadcast_to(kn2, kn_scr.shape)
        r = lax.broadcasted_iota(jnp.int32, (t, t), 0)
        c = lax.broadcasted_iota(jnp.int32, (t, t), 1)
        rel = (r - c).astype(F32) * slope
        rel_scr[...] = rel
        diag_scr[...] = jnp.where(c <= r, rel, jnp.inf)

    bound = jnp.sqrt(max_sq_norm(q_ref[...]) * kn_scr[0:1, :]) * NORM_BOUND_MARGIN + NORM_BOUND_SLACK

    def slack():
        gaps = [bound[:, mp:mp + 1] - jnp.min(m_scr[mp], axis=0, keepdims=True)[:, 0:1]
                for mp in range(len(maps))]
        return functools.reduce(jnp.maximum, gaps)[0, 0]

    diagonal_chunk()

    def live(j, slk):
        nearest = ((i - j - 1) * t + 1).astype(F32)
        return jnp.logical_and(j >= 0, slk - slope * nearest >= DIFF_DEAD_LOG2)

    def cond(st):
        return live(*st)

    def keep_two(j, slk):
        chunks_keep_max((j, j - 1))
        return j - 2, slk

    def keep_one(j, slk):
        chunks_keep_max((j,))
        return j - 1, slk

    def keep_four(j, slk):
        chunks_keep_max((j, j - 1, j - 2, j - 3))
        return j - 4, slk

    def keep_few(j, slk):
        return lax.cond(live(j - 1, slk), keep_two, keep_one, j, slk)

    def keep_max(j, slk):
        return lax.cond(live(j - 3, slk), keep_four, keep_few, j, slk)

    def update_max(j, slk):
        update_chunk(j)
        return j - 1, slack()

    def body(st):
        j, slk = st
        return lax.cond(slk <= DIFF_EXP_HEADROOM, keep_max, update_max, j, slk)

    lax.while_loop(cond, body, (i - 1, slack()))

    lam = lam_ref[0:1, :]
    l0 = jnp.sum(l_scr[0], axis=-1, keepdims=True)
    l1 = jnp.sum(l_scr[1], axis=-1, keepdims=True)
    o = a_scr[0] / l0 - lam * (a_scr[1] / l1)
    o_ref[...] = (_rms(o, subln_ref[...]) * out_scale).astype(o_ref.dtype)


def _diff_attn(proj, lam, subln, lambda_init):
    bsz, s, _ = proj.shape
    t = DIFF_T
    nh = DIFF_HEADS
    slopes = tuple(2.0 ** (-8.0 * (h + 1) / nh) for h in range(nh))
    return pl.pallas_call(
        functools.partial(_diff_attn_kernel, t=t, slopes=slopes, out_scale=1.0 - lambda_init),
        grid=(bsz, nh, s // t),
        in_specs=[
            pl.BlockSpec((None, t, LANES), lambda b, h, i: (b, i, h)),
            pl.BlockSpec((None, s, LANES), lambda b, h, i: (b, 0, nh + h)),
            pl.BlockSpec((None, s, LANES), lambda b, h, i: (b, 0, 2 * nh + h)),
            pl.BlockSpec((8, LANES), lambda b, h, i: (0, 0)),
            pl.BlockSpec((1, LANES), lambda b, h, i: (0, 0)),
        ],
        out_specs=pl.BlockSpec((None, t, LANES), lambda b, h, i: (b, i, h)),
        out_shape=jax.ShapeDtypeStruct((bsz, s, nh * LANES), BF16),
        scratch_shapes=[pltpu.VMEM((8, LANES), F32)] + [pltpu.VMEM((t, t), F32)] * 2
                       + [pltpu.VMEM((2, t, LANES), F32)] * 3,
        compiler_params=_cparams(("arbitrary", "arbitrary", "arbitrary")),
        name="diff_attn",
    )(proj, proj, proj, lam, subln.reshape(1, LANES))


def _sb_attn_kernel(q_ref, k_ref, v_ref, o_ref, *, t):
    i = pl.program_id(2)
    pair_cols = [slice(p * LANES, (p + 1) * LANES) for p in range(q_ref.shape[1] // LANES)]
    queries = [_split_heads(q_ref[:, cols]) for cols in pair_cols]
    n_heads = 2 * len(pair_cols)
    r = lax.broadcasted_iota(jnp.int32, (t, t), 0)
    c = lax.broadcasted_iota(jnp.int32, (t, t), 1)
    strict = c < r
    after = (r > c).astype(BF16)
    after2 = jnp.concatenate([after, after], axis=0)

    def prefix(j, *, diagonal, valid=None):
        start = pl.multiple_of(j * t, t)
        values, zs = [], []
        for cols, pair in zip(pair_cols, queries):
            k = k_ref[pl.ds(start, t), cols]
            v = v_ref[pl.ds(start, t), cols]
            if valid is not None:
                v = jnp.where(valid, v, jnp.zeros_like(v))
            values.extend(_split_heads(v))
            zs.extend(_dot_nt(qm, k) for qm in pair)
        log_betas, splits, firsts = [], [], []
        for z in zs:
            soft = jnp.log2(1.0 + jnp.exp2(-jnp.abs(z)))
            log_beta = jnp.minimum(z, 0.0) - soft
            x = log_beta - z
            if diagonal:
                x = jnp.where(strict, x, 0.0)
            xh = x.astype(BF16)
            xl = (x - xh.astype(F32)).astype(BF16)
            log_betas.append(log_beta)
            splits.append(jnp.concatenate([xh, xl], axis=1))
            firsts.append(x[:, 0:1])
        inners = [jnp.dot(sp, after2, preferred_element_type=F32) for sp in splits]
        parts = []
        for log_beta, inner, first in zip(log_betas, inners, firsts):
            total = inner[:, 0:1] + first
            if valid is not None:
                total = jnp.where(valid, total, 0.0)
            parts.append((log_beta, inner, total))
        return parts, values

    def finish(parts, values, carries, *, diagonal):
        atts = []
        for (log_beta, inner, _), carry in zip(parts, carries):
            att = jnp.exp2(log_beta + (inner + carry))
            if diagonal:
                att = jnp.where(strict, att, 0.0)
            atts.append(att.astype(BF16))
        outs = [jnp.dot(att, vm, preferred_element_type=F32) for att, vm in zip(atts, values)]
        new_carries = [carry + total for (_, _, total), carry in zip(parts, carries)]
        return new_carries, [outs[h] + outs[h + 1] for h in range(0, n_heads, 2)]

    has_prev = i > 0
    zero_col = jnp.zeros((t, 1), F32)
    diag_parts, diag_values = prefix(i, diagonal=True)
    prev_parts, prev_values = prefix(jnp.maximum(i - 1, 0), diagonal=False, valid=has_prev)
    carries, accs = finish(diag_parts, diag_values, [zero_col] * n_heads, diagonal=True)
    carries, accs_prev = finish(prev_parts, prev_values, carries, diagonal=False)
    accs = [a + b for a, b in zip(accs, accs_prev)]

    def alive(carries):
        return jnp.max(functools.reduce(jnp.maximum, carries)) > SB_DEAD_LOG2

    def cond(state):
        return jnp.logical_and(state[0] >= 0, state[-1])

    def body(state):
        j, carries, accs, _ = state
        parts, values = prefix(j, diagonal=False)
        carries, outs = finish(parts, values, carries, diagonal=False)
        return j - 1, carries, [a + b for a, b in zip(accs, outs)], alive(carries)

    _, _, accs, _ = lax.while_loop(cond, body, (i - 2, carries, accs, alive(carries)))
    for cols, acc in zip(pair_cols, accs):
        o_ref[:, cols] = acc


def _sb_attn(proj):
    bsz, s, _ = proj.shape
    t = SB_T
    pairs = SB_HEADS * HEAD_DIM // LANES
    width = SB_PAIRS_PER_STEP * LANES
    groups = pairs // SB_PAIRS_PER_STEP
    base = 3 * DIFF_HEADS // SB_PAIRS_PER_STEP
    whole_seq = lambda index_map: pl.BlockSpec((None, s, width), index_map, pipeline_mode=pl.Buffered(1))
    return pl.pallas_call(
        functools.partial(_sb_attn_kernel, t=t),
        grid=(bsz, groups, s // t),
        in_specs=[
            pl.BlockSpec((None, t, width), lambda b, g, i: (b, i, base + g)),
            whole_seq(lambda b, g, i: (b, 0, base + groups + g)),
            whole_seq(lambda b, g, i: (b, 0, base + 2 * groups + g)),
        ],
        out_specs=pl.BlockSpec((None, t, width), lambda b, g, i: (b, i, g)),
        out_shape=jax.ShapeDtypeStruct((bsz, s, pairs * LANES), F32),
        compiler_params=_cparams(("parallel", "parallel", "arbitrary")),
        name="sb_attn",
    )(proj, proj, proj)


def _mix_out_ffn_kernel(x_ref, od_ref, os_ref, mgt_ref, beta_ref, mgpost_ref, wod_ref, wos_ref,
                        sh_ref, sc_ref, gt_ref, gpre_ref, gpost_ref, wg_ref, wu_ref, wd_ref,
                        o_ref, x_scr, h_scr, a_scr, y_scr, *, tf):
    osn = _rms(os_ref[...], beta_ref[...]).astype(BF16)
    y = (jnp.dot(od_ref[...], wod_ref[...], preferred_element_type=F32)
         + jnp.dot(osn, wos_ref[...], preferred_element_type=F32))
    x_scr[...] = x_ref[...] + MIX_RES_WEIGHT * mgt_ref[...] * _rms(y, mgpost_ref[...])
    o_ref[...] = _swiglu_sublayer(x_scr, sh_ref[...], sc_ref[...], gt_ref[...], gpre_ref[...], gpost_ref[...],
                                  wg_ref, wu_ref, wd_ref, h_scr, a_scr, y_scr, tf)


def _mix_out_ffn(x, o_d, o_s, mod4, k_gate, sb_beta, mix_g_post, w_out, k0, g_pre, g_post, w_gate, w_up, w_down):
    bsz, s, d = x.shape
    dd, ds = o_d.shape[-1], o_s.shape[-1]
    d_ff = w_gate.shape[1]
    tm, tf = FFN_TM, FFN_TF
    w_bf = w_out.astype(BF16)
    tile = lambda n: pl.BlockSpec((None, tm, n), lambda b, i: (b, i, 0))
    mod_spec = lambda k: pl.BlockSpec((None, None, 1, d), lambda b, i: (b, k, 0, 0))
    whole = lambda r, c: _resident((r, c), lambda b, i: (0, 0))
    return pl.pallas_call(
        functools.partial(_mix_out_ffn_kernel, tf=tf),
        grid=(bsz, s // tm),
        in_specs=[
            tile(d), tile(dd), tile(ds),
            mod_spec(k_gate), whole(1, ds), whole(1, d), whole(dd, d), whole(ds, d),
            mod_spec(k0), mod_spec(k0 + 1), mod_spec(k0 + 2), whole(1, d), whole(1, d),
            whole(d, d_ff), whole(d, d_ff), whole(d_ff, d),
        ],
        out_specs=tile(d),
        out_shape=jax.ShapeDtypeStruct(x.shape, F32),
        scratch_shapes=[pltpu.VMEM((tm, d), F32), pltpu.VMEM((tm, d), BF16), pltpu.VMEM((tm, d_ff), BF16),
                        pltpu.VMEM((tm, d), F32)],
        compiler_params=_cparams(("parallel", "parallel")),
        name="mix_out_ffn",
    )(x, o_d, o_s, mod4, sb_beta.reshape(1, ds), mix_g_post.reshape(1, d), w_bf[:dd], w_bf[dd:],
      mod4, mod4, mod4, g_pre.reshape(1, d), g_post.reshape(1, d),
      w_gate.astype(BF16), w_up.astype(BF16), w_down.astype(BF16))


def kernel(x, c, w_ada, b_ada, ffn1_g_pre, ffn1_g_post, ffn1_w_gate, ffn1_w_up, ffn1_w_down, mix_g_pre, mix_g_post, w_in, w_out, lam_q1, lam_k1, lam_q2, lam_k2, diff_subln, sb_beta, ffn2_g_pre, ffn2_g_post, ffn2_w_gate, ffn2_w_up, ffn2_w_down):
    bsz, _, d = x.shape
    depth = w_ada.shape[0]
    for l in range(depth):
        lambda_init = 0.8 - 0.6 * math.exp(-0.3 * l)
        lam_params = jnp.stack([lam_q1[l], lam_k1[l], lam_q2[l], lam_k2[l]])
        mod, lam = _adaln(c, w_ada[l], b_ada[l], lam_params, lambda_init)
        mod4 = mod.reshape(bsz, N_MOD, 1, d)
        x = _ffn(x, mod4, 0, ffn1_g_pre[l], ffn1_g_post[l], ffn1_w_gate[l], ffn1_w_up[l], ffn1_w_down[l])
        proj = _mix_in(x, mod4, 3, mix_g_pre[l], w_in[l])
        o_d = _diff_attn(proj, lam, diff_subln[l], lambda_init)
        o_s = _sb_attn(proj)
        x = _mix_out_ffn(x, o_d, o_s, mod4, 5, sb_beta[l], mix_g_post[l], w_out[l],
                         6, ffn2_g_pre[l], ffn2_g_post[l], ffn2_w_gate[l], ffn2_w_up[l], ffn2_w_down[l])
    return x
```

```python
import functools
import math

import jax
import jax.numpy as jnp
from jax import lax
from jax.experimental import pallas as pl
from jax.experimental.pallas import tpu as pltpu

F32 = jnp.float32
BF16 = jnp.bfloat16

HEAD_DIM = 64
DIFF_HEADS = 4
SB_HEADS = 8
N_MOD = 9
RMS_EPS = 1e-6
FFN_RES_WEIGHT = 0.5
MIX_RES_WEIGHT = 1.0

LANES = 128
VMEM_LIMIT_BYTES = 56 * 1024 * 1024

ADA_TN = 1024
FFN_TM = 512
FFN_TF = 256
DIFF_T = 512
SB_T = 256
SB_PAIRS_PER_STEP = 4

LOG2E = math.log2(math.e)
DIFF_Q_SCALE = LOG2E * HEAD_DIM ** -0.5
SB_Q_SCALE = LOG2E * HEAD_DIM ** -0.5

SB_DEAD_LOG2 = -160.0
DIFF_DEAD_LOG2 = -150.0
DIFF_EXP_HEADROOM = 100.0
NORM_BOUND_MARGIN = 1.01
NORM_BOUND_SLACK = 1.0


def _cparams(semantics):
    return pltpu.CompilerParams(dimension_semantics=semantics, vmem_limit_bytes=VMEM_LIMIT_BYTES)


def _resident(block_shape, index_map):
    return pl.BlockSpec(block_shape, index_map, pipeline_mode=pl.Buffered(1))


def _sigmoid(x):
    return 1.0 / (1.0 + jnp.exp(-x))


def _rms(x, gain):
    ms = jnp.mean(x * x, axis=-1, keepdims=True)
    return x * lax.rsqrt(ms + RMS_EPS) * gain


def _adaln_kernel(c_ref, w_ref, b_ref, lamp_ref, mod_ref, lam_ref, *, lambda_init):
    c = c_ref[...]
    s = c * _sigmoid(c)
    mod_ref[...] = jnp.dot(s, w_ref[...], preferred_element_type=F32,
                           precision=lax.Precision.HIGHEST) + b_ref[...]
    lp = lamp_ref[...]
    d1 = jnp.sum(lp[0:1] * lp[1:2], axis=-1, keepdims=True)
    d2 = jnp.sum(lp[2:3] * lp[3:4], axis=-1, keepdims=True)
    lam = jnp.exp(d1) - jnp.exp(d2) + lambda_init
    lam_ref[...] = jnp.broadcast_to(lam, lam_ref.shape)


def _adaln(c, w_ada, b_ada, lam_params, lambda_init):
    bsz, d = c.shape
    n = w_ada.shape[1]
    tn = ADA_TN
    return pl.pallas_call(
        functools.partial(_adaln_kernel, lambda_init=lambda_init),
        grid=(n // tn,),
        in_specs=[
            pl.BlockSpec((bsz, d), lambda j: (0, 0)),
            pl.BlockSpec((d, tn), lambda j: (0, j)),
            pl.BlockSpec((1, tn), lambda j: (0, j)),
            pl.BlockSpec(lam_params.shape, lambda j: (0, 0)),
        ],
        out_specs=[
            pl.BlockSpec((bsz, tn), lambda j: (0, j)),
            pl.BlockSpec((8, LANES), lambda j: (0, 0)),
        ],
        out_shape=[
            jax.ShapeDtypeStruct((bsz, n), F32),
            jax.ShapeDtypeStruct((8, LANES), F32),
        ],
        compiler_params=_cparams(("arbitrary",)),
        name="adaln",
    )(c, w_ada, b_ada.reshape(1, n), lam_params)


def _modulated_norm(x, g_pre, scale, shift):
    return _rms(x, g_pre) * (1.0 + scale) + shift


def _swiglu_sublayer(x_ref, sh, sc, gt, g_pre, g_post, wg_ref, wu_ref, wd_ref, h_scr, a_scr, y_scr, tf):
    x = x_ref[...]
    inv = lax.rsqrt(jnp.mean(x * x, axis=-1, keepdims=True) + RMS_EPS)
    h_scr[...] = (x * inv * (g_pre * (1.0 + sc)) + sh).astype(BF16)
    d_ff = wg_ref.shape[1]
    for j in range(d_ff // tf):
        cols = slice(j * tf, (j + 1) * tf)
        h = h_scr[...]
        g = jnp.dot(h, wg_ref[:, cols], preferred_element_type=F32)
        u = jnp.dot(h, wu_ref[:, cols], preferred_element_type=F32)
        a_scr[:, cols] = (g * _sigmoid(g) * u).astype(BF16)
    d = wd_ref.shape[1]
    sq = None
    for n in range(d // tf):
        cols = slice(n * tf, (n + 1) * tf)
        y = jnp.dot(a_scr[...], wd_ref[:, cols], preferred_element_type=F32)
        y_scr[:, cols] = y
        part = jnp.sum(y * y, axis=-1, keepdims=True)
        sq = part if sq is None else sq + part
    inv = lax.rsqrt(sq * (1.0 / d) + RMS_EPS)
    return x_ref[...] + y_scr[...] * inv * (FFN_RES_WEIGHT * gt * g_post)


def _ffn_kernel(x_ref, sh_ref, sc_ref, gt_ref, gpre_ref, gpost_ref, wg_ref, wu_ref, wd_ref,
                o_ref, h_scr, a_scr, y_scr, *, tf):
    o_ref[...] = _swiglu_sublayer(x_ref, sh_ref[...], sc_ref[...], gt_ref[...], gpre_ref[...], gpost_ref[...],
                                  wg_ref, wu_ref, wd_ref, h_scr, a_scr, y_scr, tf)


def _ffn(x, mod4, k0, g_pre, g_post, w_gate, w_up, w_down):
    bsz, s, d = x.shape
    d_ff = w_gate.shape[1]
    tm, tf = FFN_TM, FFN_TF
    mod_spec = lambda k: pl.BlockSpec((None, None, 1, d), lambda b, i: (b, k, 0, 0))
    row = lambda: _resident((1, d), lambda b, i: (0, 0))
    return pl.pallas_call(
        functools.partial(_ffn_kernel, tf=tf),
        grid=(bsz, s // tm),
        in_specs=[
            pl.BlockSpec((None, tm, d), lambda b, i: (b, i, 0)),
            mod_spec(k0), mod_spec(k0 + 1), mod_spec(k0 + 2),
            row(), row(),
            _resident((d, d_ff), lambda b, i: (0, 0)),
            _resident((d, d_ff), lambda b, i: (0, 0)),
            _resident((d_ff, d), lambda b, i: (0, 0)),
        ],
        out_specs=pl.BlockSpec((None, tm, d), lambda b, i: (b, i, 0)),
        out_shape=jax.ShapeDtypeStruct(x.shape, F32),
        scratch_shapes=[pltpu.VMEM((tm, d), BF16), pltpu.VMEM((tm, d_ff), BF16), pltpu.VMEM((tm, d), F32)],
        compiler_params=_cparams(("parallel", "parallel")),
        name="ffn",
    )(x, mod4, mod4, mod4, g_pre.reshape(1, d), g_post.reshape(1, d),
      w_gate.astype(BF16), w_up.astype(BF16), w_down.astype(BF16))


def _mix_in_kernel(x_ref, sh_ref, sc_ref, gpre_ref, w_ref, o_ref, h_scr, *, group, group_scale):
    h_scr[...] = _modulated_norm(x_ref[...], gpre_ref[...], sc_ref[...], sh_ref[...]).astype(BF16)
    for g in range(w_ref.shape[1] // group):
        cols = slice(g * group, (g + 1) * group)
        p = jnp.dot(h_scr[...], w_ref[:, cols], preferred_element_type=F32)
        if g in group_scale:
            p = p * group_scale[g]
        o_ref[:, cols] = p.astype(BF16)


def _mix_in(x, mod4, k0, g_pre, w_in):
    bsz, s, d = x.shape
    n = w_in.shape[1]
    group = n // 6
    tm = FFN_TM
    mod_spec = lambda k: pl.BlockSpec((None, None, 1, d), lambda b, i: (b, k, 0, 0))
    return pl.pallas_call(
        functools.partial(_mix_in_kernel, group=group,
                          group_scale={0: DIFF_Q_SCALE, 3: SB_Q_SCALE}),
        grid=(bsz, s // tm),
        in_specs=[
            pl.BlockSpec((None, tm, d), lambda b, i: (b, i, 0)),
            mod_spec(k0), mod_spec(k0 + 1),
            _resident((1, d), lambda b, i: (0, 0)),
            _resident((d, n), lambda b, i: (0, 0)),
        ],
        out_specs=pl.BlockSpec((None, tm, n), lambda b, i: (b, i, 0)),
        out_shape=jax.ShapeDtypeStruct((bsz, s, n), BF16),
        scratch_shapes=[pltpu.VMEM((tm, d), BF16)],
        compiler_params=_cparams(("parallel", "parallel")),
        name="mix_in",
    )(x, mod4, mod4, g_pre.reshape(1, d), w_in.astype(BF16))


def _dot_nt(a, b):
    return lax.dot_general(a, b, (((1,), (1,)), ((), ())), preferred_element_type=F32)


def _split_heads(x):
    lane = lax.broadcasted_iota(jnp.int32, x.shape, 1)
    zero = jnp.zeros_like(x)
    return jnp.where(lane < HEAD_DIM, x, zero), jnp.where(lane >= HEAD_DIM, x, zero)


def _diff_attn_kernel(q_ref, k_ref, v_ref, lam_ref, subln_ref, o_ref, kn_scr, rel_scr, diag_scr,
                      m_scr, l_scr, a_scr, *, t, slopes, out_scale):
    head = pl.program_id(1)
    i = pl.program_id(2)
    slope = jnp.float32(slopes[-1])
    for hh in range(len(slopes) - 2, -1, -1):
        slope = jnp.where(head == hh, jnp.float32(slopes[hh]), slope)
    slope = slope * LOG2E

    maps = _split_heads(q_ref[...])
    lane_tiles = t // LANES

    def lane_partial_sum(p):
        parts = [p[:, n * LANES:(n + 1) * LANES] for n in range(p.shape[1] // LANES)]
        return functools.reduce(lambda a, b: a + b, parts)

    def load_chunk(j):
        start = pl.multiple_of(j * t, t)
        off = slope * ((i - j) * t).astype(F32)
        return k_ref[pl.ds(start, t), :], v_ref[pl.ds(start, t), :], off

    def diagonal_chunk():
        k, v, _ = load_chunk(i)
        for mp, qm in enumerate(maps):
            s = _dot_nt(qm, k) - diag_scr[...]
            m_new = jnp.broadcast_to(jnp.max(s, axis=-1, keepdims=True), (t, LANES))
            p = jnp.exp2(s - jnp.tile(m_new, (1, lane_tiles)))
            m_scr[mp] = m_new
            l_scr[mp] = lane_partial_sum(p)
            a_scr[mp] = jnp.dot(p.astype(BF16), v, preferred_element_type=F32)

    def update_chunk(j):
        k, v, off = load_chunk(j)
        for mp, qm in enumerate(maps):
            s = _dot_nt(qm, k) - rel_scr[...]
            m = m_scr[mp]
            m_new = jnp.maximum(m, jnp.max(s, axis=-1, keepdims=True) - off)
            alpha = jnp.exp2(m - m_new)
            p = jnp.exp2(s - jnp.tile(m_new + off, (1, lane_tiles)))
            m_scr[mp] = m_new
            l_scr[mp] = alpha * l_scr[mp] + lane_partial_sum(p)
            a_scr[mp] = alpha * a_scr[mp] + jnp.dot(p.astype(BF16), v, preferred_element_type=F32)

    def chunks_keep_max(js):
        loaded = [load_chunk(j) for j in js]
        for mp, qm in enumerate(maps):
            m = m_scr[mp]
            l_new = a_new = None
            for k, v, off in loaded:
                p = jnp.exp2(_dot_nt(qm, k) - rel_scr[...] - jnp.tile(m + off, (1, lane_tiles)))
                ls = lane_partial_sum(p)
                av = jnp.dot(p.astype(BF16), v, preferred_element_type=F32)
                l_new = ls if l_new is None else l_new + ls
                a_new = av if a_new is None else a_new + av
            l_scr[mp] = l_scr[mp] + l_new
            a_scr[mp] = a_scr[mp] + a_new

    sel_r = lax.broadcasted_iota(jnp.int32, (LANES, LANES), 0)
    sel_c = lax.broadcasted_iota(jnp.int32, (LANES, LANES), 1)
    sel = jnp.where(sel_c == lax.shift_right_logical(sel_r, HEAD_DIM.bit_length() - 1), 1.0, 0.0).astype(BF16)

    def max_sq_norm(x):
        xf = x.astype(F32)
        n2 = jnp.dot((xf * xf).astype(BF16), sel, preferred_element_type=F32)
        return jnp.max(n2, axis=0, keepdims=True)

    @pl.when(i == 0)
    def _():
        def body(jj, best):
            kk = k_ref[pl.ds(pl.multiple_of(jj * t, t), t), :]
            return jnp.maximum(best, max_sq_norm(kk))
        kn2 = lax.fori_loop(0, k_ref.shape[0] // t, body, jnp.zeros((1, LANES), F32), unroll=True)
        kn_scr[...] = jnp.broadcast_to(kn2, kn_scr.shape)
        r = lax.broadcasted_iota(jnp.int32, (t, t), 0)
        c = lax.broadcasted_iota(jnp.int32, (t, t), 1)
        rel = (r - c).astype(F32) * slope
        rel_scr[...] = rel
        diag_scr[...] = jnp.where(c <= r, rel, jnp.inf)

    bound = jnp.sqrt(max_sq_norm(q_ref[...]) * kn_scr[0:1, :]) * NORM_BOUND_MARGIN + NORM_BOUND_SLACK

    def slack():
        gaps = [bound[:, mp:mp + 1] - jnp.min(m_scr[mp], axis=0, keepdims=True)[:, 0:1]
                for mp in range(len(maps))]
        return functools.reduce(jnp.maximum, gaps)[0, 0]

    diagonal_chunk()

    def live(j, slk):
        nearest = ((i - j - 1) * t + 1).astype(F32)
        return jnp.logical_and(j >= 0, slk - slope * nearest >= DIFF_DEAD_LOG2)

    def cond(st):
        return live(*st)

    def keep_two(j, slk):
        chunks_keep_max((j, j - 1))
        return j - 2, slk

    def keep_one(j, slk):
        chunks_keep_max((j,))
        return j - 1, slk

    def keep_four(j, slk):
        chunks_keep_max((j, j - 1, j - 2, j - 3))
        return j - 4, slk

    def keep_few(j, slk):
        return lax.cond(live(j - 1, slk), keep_two, keep_one, j, slk)

    def keep_max(j, slk):
        return lax.cond(live(j - 3, slk), keep_four, keep_few, j, slk)

    def update_max(j, slk):
        update_chunk(j)
        return j - 1, slack()

    def body(st):
        j, slk = st
        return lax.cond(slk <= DIFF_EXP_HEADROOM, keep_max, update_max, j, slk)

    lax.while_loop(cond, body, (i - 1, slack()))

    lam = lam_ref[0:1, :]
    l0 = jnp.sum(l_scr[0], axis=-1, keepdims=True)
    l1 = jnp.sum(l_scr[1], axis=-1, keepdims=True)
    o = a_scr[0] / l0 - lam * (a_scr[1] / l1)
    o_ref[...] = (_rms(o, subln_ref[...]) * out_scale).astype(o_ref.dtype)


def _diff_attn(proj, lam, subln, lambda_init):
    bsz, s, _ = proj.shape
    t = DIFF_T
    nh = DIFF_HEADS
    slopes = tuple(2.0 ** (-8.0 * (h + 1) / nh) for h in range(nh))
    return pl.pallas_call(
        functools.partial(_diff_attn_kernel, t=t, slopes=slopes, out_scale=1.0 - lambda_init),
        grid=(bsz, nh, s // t),
        in_specs=[
            pl.BlockSpec((None, t, LANES), lambda b, h, i: (b, i, h)),
            pl.BlockSpec((None, s, LANES), lambda b, h, i: (b, 0, nh + h)),
            pl.BlockSpec((None, s, LANES), lambda b, h, i: (b, 0, 2 * nh + h)),
            pl.BlockSpec((8, LANES), lambda b, h, i: (0, 0)),
            pl.BlockSpec((1, LANES), lambda b, h, i: (0, 0)),
        ],
        out_specs=pl.BlockSpec((None, t, LANES), lambda b, h, i: (b, i, h)),
        out_shape=jax.ShapeDtypeStruct((bsz, s, nh * LANES), BF16),
        scratch_shapes=[pltpu.VMEM((8, LANES), F32)] + [pltpu.VMEM((t, t), F32)] * 2
                       + [pltpu.VMEM((2, t, LANES), F32)] * 3,
        compiler_params=_cparams(("arbitrary", "arbitrary", "arbitrary")),
        name="diff_attn",
    )(proj, proj, proj, lam, subln.reshape(1, LANES))


def _sb_attn_kernel(q_ref, k_ref, v_ref, o_ref, *, t):
    i = pl.program_id(2)
    pair_cols = [slice(p * LANES, (p + 1) * LANES) for p in range(q_ref.shape[1] // LANES)]
    queries = [_split_heads(q_ref[:, cols]) for cols in pair_cols]
    n_heads = 2 * len(pair_cols)
    r = lax.broadcasted_iota(jnp.int32, (t, t), 0)
    c = lax.broadcasted_iota(jnp.int32, (t, t), 1)
    strict = c < r
    after = (r > c).astype(BF16)
    after2 = jnp.concatenate([after, after], axis=0)

    def prefix(j, *, diagonal, valid=None):
        start = pl.multiple_of(j * t, t)
        values, zs = [], []
        for cols, pair in zip(pair_cols, queries):
            k = k_ref[pl.ds(start, t), cols]
            v = v_ref[pl.ds(start, t), cols]
            if valid is not None:
                v = jnp.where(valid, v, jnp.zeros_like(v))
            values.extend(_split_heads(v))
            zs.extend(_dot_nt(qm, k) for qm in pair)
        log_betas, splits, firsts = [], [], []
        for z in zs:
            soft = jnp.log2(1.0 + jnp.exp2(-jnp.abs(z)))
            log_beta = jnp.minimum(z, 0.0) - soft
            x = log_beta - z
            if diagonal:
                x = jnp.where(strict, x, 0.0)
            xh = x.astype(BF16)
            xl = (x - xh.astype(F32)).astype(BF16)
            log_betas.append(log_beta)
            splits.append(jnp.concatenate([xh, xl], axis=1))
            firsts.append(x[:, 0:1])
        inners = [jnp.dot(sp, after2, preferred_element_type=F32) for sp in splits]
        parts = []
        for log_beta, inner, first in zip(log_betas, inners, firsts):
            total = inner[:, 0:1] + first
            if valid is not None:
                total = jnp.where(valid, total, 0.0)
            parts.append((log_beta, inner, total))
        return parts, values

    def finish(parts, values, carries, *, diagonal):
        atts = []
        for (log_beta, inner, _), carry in zip(parts, carries):
            att = jnp.exp2(log_beta + (inner + carry))
            if diagonal:
                att = jnp.where(strict, att, 0.0)
            atts.append(att.astype(BF16))
        outs = [jnp.dot(att, vm, preferred_element_type=F32) for att, vm in zip(atts, values)]
        new_carries = [carry + total for (_, _, total), carry in zip(parts, carries)]
        return new_carries, [outs[h] + outs[h + 1] for h in range(0, n_heads, 2)]

    has_prev = i > 0
    zero_col = jnp.zeros((t, 1), F32)
    diag_parts, diag_values = prefix(i, diagonal=True)
    prev_parts, prev_values = prefix(jnp.maximum(i - 1, 0), diagonal=False, valid=has_prev)
    carries, accs = finish(diag_parts, diag_values, [zero_col] * n_heads, diagonal=True)
    carries, accs_prev = finish(prev_parts, prev_values, carries, diagonal=False)
    accs = [a + b for a, b in zip(accs, accs_prev)]

    def alive(carries):
        return jnp.max(functools.reduce(jnp.maximum, carries)) > SB_DEAD_LOG2

    def cond(state):
        return jnp.logical_and(state[0] >= 0, state[-1])

    def body(state):
        j, carries, accs, _ = state
        parts, values = prefix(j, diagonal=False)
        carries, outs = finish(parts, values, carries, diagonal=False)
        return j - 1, carries, [a + b for a, b in zip(accs, outs)], alive(carries)

    _, _, accs, _ = lax.while_loop(cond, body, (i - 2, carries, accs, alive(carries)))
    for cols, acc in zip(pair_cols, accs):
        o_ref[:, cols] = acc


def _sb_attn(proj):
    bsz, s, _ = proj.shape
    t = SB_T
    pairs = SB_HEADS * HEAD_DIM // LANES
    width = SB_PAIRS_PER_STEP * LANES
    groups = pairs // SB_PAIRS_PER_STEP
    base = 3 * DIFF_HEADS // SB_PAIRS_PER_STEP
    whole_seq = lambda index_map: pl.BlockSpec((None, s, width), index_map, pipeline_mode=pl.Buffered(1))
    return pl.pallas_call(
        functools.partial(_sb_attn_kernel, t=t),
        grid=(bsz, groups, s // t),
        in_specs=[
            pl.BlockSpec((None, t, width), lambda b, g, i: (b, i, base + g)),
            whole_seq(lambda b, g, i: (b, 0, base + groups + g)),
            whole_seq(lambda b, g, i: (b, 0, base + 2 * groups + g)),
        ],
        out_specs=pl.BlockSpec((None, t, width), lambda b, g, i: (b, i, g)),
        out_shape=jax.ShapeDtypeStruct((bsz, s, pairs * LANES), F32),
        compiler_params=_cparams(("parallel", "parallel", "arbitrary")),
        name="sb_attn",
    )(proj, proj, proj)


def _mix_out_ffn_kernel(x_ref, od_ref, os_ref, mgt_ref, beta_ref, mgpost_ref, wod_ref, wos_ref,
                        sh_ref, sc_ref, gt_ref, gpre_ref, gpost_ref, wg_ref, wu_ref, wd_ref,
                        o_ref, x_scr, h_scr, a_scr, y_scr, *, tf):
    osn = _rms(os_ref[...], beta_ref[...]).astype(BF16)
    y = (jnp.dot(od_ref[...], wod_ref[...], preferred_element_type=F32)
         + jnp.dot(osn, wos_ref[...], preferred_element_type=F32))
    x_scr[...] = x_ref[...] + MIX_RES_WEIGHT * mgt_ref[...] * _rms(y, mgpost_ref[...])
    o_ref[...] = _swiglu_sublayer(x_scr, sh_ref[...], sc_ref[...], gt_ref[...], gpre_ref[...], gpost_ref[...],
                                  wg_ref, wu_ref, wd_ref, h_scr, a_scr, y_scr, tf)


def _mix_out_ffn(x, o_d, o_s, mod4, k_gate, sb_beta, mix_g_post, w_out, k0, g_pre, g_post, w_gate, w_up, w_down):
    bsz, s, d = x.shape
    dd, ds = o_d.shape[-1], o_s.shape[-1]
    d_ff = w_gate.shape[1]
    tm, tf = FFN_TM, FFN_TF
    w_bf = w_out.astype(BF16)
    tile = lambda n: pl.BlockSpec((None, tm, n), lambda b, i: (b, i, 0))
    mod_spec = lambda k: pl.BlockSpec((None, None, 1, d), lambda b, i: (b, k, 0, 0))
    whole = lambda r, c: _resident((r, c), lambda b, i: (0, 0))
    return pl.pallas_call(
        functools.partial(_mix_out_ffn_kernel, tf=tf),
        grid=(bsz, s // tm),
        in_specs=[
            tile(d), tile(dd), tile(ds),
            mod_spec(k_gate), whole(1, ds), whole(1, d), whole(dd, d), whole(ds, d),
            mod_spec(k0), mod_spec(k0 + 1), mod_spec(k0 + 2), whole(1, d), whole(1, d),
            whole(d, d_ff), whole(d, d_ff), whole(d_ff, d),
        ],
        out_specs=tile(d),
        out_shape=jax.ShapeDtypeStruct(x.shape, F32),
        scratch_shapes=[pltpu.VMEM((tm, d), F32), pltpu.VMEM((tm, d), BF16), pltpu.VMEM((tm, d_ff), BF16),
                        pltpu.VMEM((tm, d), F32)],
        compiler_params=_cparams(("parallel", "parallel")),
        name="mix_out_ffn",
    )(x, o_d, o_s, mod4, sb_beta.reshape(1, ds), mix_g_post.reshape(1, d), w_bf[:dd], w_bf[dd:],
      mod4, mod4, mod4, g_pre.reshape(1, d), g_post.reshape(1, d),
      w_gate.astype(BF16), w_up.astype(BF16), w_down.astype(BF16))


def kernel(x, c, w_ada, b_ada, ffn1_g_pre, ffn1_g_post, ffn1_w_gate, ffn1_w_up, ffn1_w_down, mix_g_pre, mix_g_post, w_in, w_out, lam_q1, lam_k1, lam_q2, lam_k2, diff_subln, sb_beta, ffn2_g_pre, ffn2_g_post, ffn2_w_gate, ffn2_w_up, ffn2_w_down):
    bsz, _, d = x.shape
    depth = w_ada.shape[0]
    for l in range(depth):
        lambda_init = 0.8 - 0.6 * math.exp(-0.3 * l)
        lam_params = jnp.stack([lam_q1[l], lam_k1[l], lam_q2[l], lam_k2[l]])
        mod, lam = _adaln(c, w_ada[l], b_ada[l], lam_params, lambda_init)
        mod4 = mod.reshape(bsz, N_MOD, 1, d)
        x = _ffn(x, mod4, 0, ffn1_g_pre[l], ffn1_g_post[l], ffn1_w_gate[l], ffn1_w_up[l], ffn1_w_down[l])
        proj = _mix_in(x, mod4, 3, mix_g_pre[l], w_in[l])
        o_d = _diff_attn(proj, lam, diff_subln[l], lambda_init)
        o_s = _sb_attn(proj)
        x = _mix_out_ffn(x, o_d, o_s, mod4, 5, sb_beta[l], mix_g_post[l], w_out[l],
                         6, ffn2_g_pre[l], ffn2_g_post[l], ffn2_w_gate[l], ffn2_w_up[l], ffn2_w_down[l])
    return x
```

```python
import functools
import math

import jax
import jax.numpy as jnp
from jax import lax
from jax.experimental import pallas as pl
from jax.experimental.pallas import tpu as pltpu

F32 = jnp.float32
BF16 = jnp.bfloat16

HEAD_DIM = 64
DIFF_HEADS = 4
SB_HEADS = 8
N_MOD = 9
RMS_EPS = 1e-6
FFN_RES_WEIGHT = 0.5
MIX_RES_WEIGHT = 1.0

LANES = 128
VMEM_LIMIT_BYTES = 56 * 1024 * 1024

ADA_TN = 1024
FFN_TM = 512
FFN_TF = 256
DIFF_T = 512
SB_T = 256
SB_PAIRS_PER_STEP = 4

LOG2E = math.log2(math.e)
DIFF_Q_SCALE = LOG2E * HEAD_DIM ** -0.5
SB_Q_SCALE = LOG2E * HEAD_DIM ** -0.5

SB_DEAD_LOG2 = -160.0
DIFF_DEAD_LOG2 = -150.0
DIFF_EXP_HEADROOM = 100.0
NORM_BOUND_MARGIN = 1.01
NORM_BOUND_SLACK = 1.0


def _cparams(semantics):
    return pltpu.CompilerParams(dimension_semantics=semantics, vmem_limit_bytes=VMEM_LIMIT_BYTES)


def _resident(block_shape, index_map):
    return pl.BlockSpec(block_shape, index_map, pipeline_mode=pl.Buffered(1))


def _sigmoid(x):
    return 1.0 / (1.0 + jnp.exp(-x))


def _rms(x, gain):
    ms = jnp.mean(x * x, axis=-1, keepdims=True)
    return x * lax.rsqrt(ms + RMS_EPS) * gain


def _adaln_kernel(c_ref, w_ref, b_ref, lamp_ref, mod_ref, lam_ref, *, lambda_init):
    c = c_ref[...]
    s = c * _sigmoid(c)
    mod_ref[...] = jnp.dot(s, w_ref[...], preferred_element_type=F32,
                           precision=lax.Precision.HIGHEST) + b_ref[...]
    lp = lamp_ref[...]
    d1 = jnp.sum(lp[0:1] * lp[1:2], axis=-1, keepdims=True)
    d2 = jnp.sum(lp[2:3] * lp[3:4], axis=-1, keepdims=True)
    lam = jnp.exp(d1) - jnp.exp(d2) + lambda_init
    lam_ref[...] = jnp.broadcast_to(lam, lam_ref.shape)


def _adaln(c, w_ada, b_ada, lam_params, lambda_init):
    bsz, d = c.shape
    n = w_ada.shape[1]
    tn = ADA_TN
    return pl.pallas_call(
        functools.partial(_adaln_kernel, lambda_init=lambda_init),
        grid=(n // tn,),
        in_specs=[
            pl.BlockSpec((bsz, d), lambda j: (0, 0)),
            pl.BlockSpec((d, tn), lambda j: (0, j)),
            pl.BlockSpec((1, tn), lambda j: (0, j)),
            pl.BlockSpec(lam_params.shape, lambda j: (0, 0)),
        ],
        out_specs=[
            pl.BlockSpec((bsz, tn), lambda j: (0, j)),
            pl.BlockSpec((8, LANES), lambda j: (0, 0)),
        ],
        out_shape=[
            jax.ShapeDtypeStruct((bsz, n), F32),
            jax.ShapeDtypeStruct((8, LANES), F32),
        ],
        compiler_params=_cparams(("arbitrary",)),
        name="adaln",
    )(c, w_ada, b_ada.reshape(1, n), lam_params)


def _modulated_norm(x, g_pre, scale, shift):
    return _rms(x, g_pre) * (1.0 + scale) + shift


def _swiglu_sublayer(x_ref, sh, sc, gt, g_pre, g_post, wg_ref, wu_ref, wd_ref, h_scr, a_scr, y_scr, tf):
    x = x_ref[...]
    inv = lax.rsqrt(jnp.mean(x * x, axis=-1, keepdims=True) + RMS_EPS)
    h_scr[...] = (x * inv * (g_pre * (1.0 + sc)) + sh).astype(BF16)
    d_ff = wg_ref.shape[1]
    for j in range(d_ff // tf):
        cols = slice(j * tf, (j + 1) * tf)
        h = h_scr[...]
        g = jnp.dot(h, wg_ref[:, cols], preferred_element_type=F32)
        u = jnp.dot(h, wu_ref[:, cols], preferred_element_type=F32)
        a_scr[:, cols] = (g * _sigmoid(g) * u).astype(BF16)
    d = wd_ref.shape[1]
    sq = None
    for n in range(d // tf):
        cols = slice(n * tf, (n + 1) * tf)
        y = jnp.dot(a_scr[...], wd_ref[:, cols], preferred_element_type=F32)
        y_scr[:, cols] = y
        part = jnp.sum(y * y, axis=-1, keepdims=True)
        sq = part if sq is None else sq + part
    inv = lax.rsqrt(sq * (1.0 / d) + RMS_EPS)
    return x_ref[...] + y_scr[...] * inv * (FFN_RES_WEIGHT * gt * g_post)


def _ffn_kernel(x_ref, sh_ref, sc_ref, gt_ref, gpre_ref, gpost_ref, wg_ref, wu_ref, wd_ref,
                o_ref, h_scr, a_scr, y_scr, *, tf):
    o_ref[...] = _swiglu_sublayer(x_ref, sh_ref[...], sc_ref[...], gt_ref[...], gpre_ref[...], gpost_ref[...],
                                  wg_ref, wu_ref, wd_ref, h_scr, a_scr, y_scr, tf)


def _ffn(x, mod4, k0, g_pre, g_post, w_gate, w_up, w_down):
    bsz, s, d = x.shape
    d_ff = w_gate.shape[1]
    tm, tf = FFN_TM, FFN_TF
    mod_spec = lambda k: pl.BlockSpec((None, None, 1, d), lambda b, i: (b, k, 0, 0))
    row = lambda: _resident((1, d), lambda b, i: (0, 0))
    return pl.pallas_call(
        functools.partial(_ffn_kernel, tf=tf),
        grid=(bsz, s // tm),
        in_specs=[
            pl.BlockSpec((None, tm, d), lambda b, i: (b, i, 0)),
            mod_spec(k0), mod_spec(k0 + 1), mod_spec(k0 + 2),
            row(), row(),
            _resident((d, d_ff), lambda b, i: (0, 0)),
            _resident((d, d_ff), lambda b, i: (0, 0)),
            _resident((d_ff, d), lambda b, i: (0, 0)),
        ],
        out_specs=pl.BlockSpec((None, tm, d), lambda b, i: (b, i, 0)),
        out_shape=jax.ShapeDtypeStruct(x.shape, F32),
        scratch_shapes=[pltpu.VMEM((tm, d), BF16), pltpu.VMEM((tm, d_ff), BF16), pltpu.VMEM((tm, d), F32)],
        compiler_params=_cparams(("parallel", "parallel")),
        name="ffn",
    )(x, mod4, mod4, mod4, g_pre.reshape(1, d), g_post.reshape(1, d),
      w_gate.astype(BF16), w_up.astype(BF16), w_down.astype(BF16))


def _mix_in_kernel(x_ref, sh_ref, sc_ref, gpre_ref, w_ref, o_ref, h_scr, *, group, group_scale):
    h_scr[...] = _modulated_norm(x_ref[...], gpre_ref[...], sc_ref[...], sh_ref[...]).astype(BF16)
    for g in range(w_ref.shape[1] // group):
        cols = slice(g * group, (g + 1) * group)
        p = jnp.dot(h_scr[...], w_ref[:, cols], preferred_element_type=F32)
        if g in group_scale:
            p = p * group_scale[g]
        o_ref[:, cols] = p.astype(BF16)


def _mix_in(x, mod4, k0, g_pre, w_in):
    bsz, s, d = x.shape
    n = w_in.shape[1]
    group = n // 6
    tm = FFN_TM
    mod_spec = lambda k: pl.BlockSpec((None, None, 1, d), lambda b, i: (b, k, 0, 0))
    return pl.pallas_call(
        functools.partial(_mix_in_kernel, group=group,
                          group_scale={0: DIFF_Q_SCALE, 3: SB_Q_SCALE}),
        grid=(bsz, s // tm),
        in_specs=[
            pl.BlockSpec((None, tm, d), lambda b, i: (b, i, 0)),
            mod_spec(k0), mod_spec(k0 + 1),
            _resident((1, d), lambda b, i: (0, 0)),
            _resident((d, n), lambda b, i: (0, 0)),
        ],
        out_specs=pl.BlockSpec((None, tm, n), lambda b, i: (b, i, 0)),
        out_shape=jax.ShapeDtypeStruct((bsz, s, n), BF16),
        scratch_shapes=[pltpu.VMEM((tm, d), BF16)],
        compiler_params=_cparams(("parallel", "parallel")),
        name="mix_in",
    )(x, mod4, mod4, g_pre.reshape(1, d), w_in.astype(BF16))


def _dot_nt(a, b):
    return lax.dot_general(a, b, (((1,), (1,)), ((), ())), preferred_element_type=F32)


def _split_heads(x):
    lane = lax.broadcasted_iota(jnp.int32, x.shape, 1)
    zero = jnp.zeros_like(x)
    return jnp.where(lane < HEAD_DIM, x, zero), jnp.where(lane >= HEAD_DIM, x, zero)


def _diff_attn_kernel(q_ref, k_ref, v_ref, lam_ref, subln_ref, o_ref, kn_scr, rel_scr, diag_scr,
                      m_scr, l_scr, a_scr, *, t, slopes, out_scale):
    head = pl.program_id(1)
    i = pl.program_id(2)
    slope = jnp.float32(slopes[-1])
    for hh in range(len(slopes) - 2, -1, -1):
        slope = jnp.where(head == hh, jnp.float32(slopes[hh]), slope)
    slope = slope * LOG2E

    maps = _split_heads(q_ref[...])
    lane_tiles = t // LANES

    def lane_partial_sum(p):
        parts = [p[:, n * LANES:(n + 1) * LANES] for n in range(p.shape[1] // LANES)]
        return functools.reduce(lambda a, b: a + b, parts)

    def load_chunk(j):
        start = pl.multiple_of(j * t, t)
        off = slope * ((i - j) * t).astype(F32)
        return k_ref[pl.ds(start, t), :], v_ref[pl.ds(start, t), :], off

    def diagonal_chunk():
        k, v, _ = load_chunk(i)
        for mp, qm in enumerate(maps):
            s = _dot_nt(qm, k) - diag_scr[...]
            m_new = jnp.broadcast_to(jnp.max(s, axis=-1, keepdims=True), (t, LANES))
            p = jnp.exp2(s - jnp.tile(m_new, (1, lane_tiles)))
            m_scr[mp] = m_new
            l_scr[mp] = lane_partial_sum(p)
            a_scr[mp] = jnp.dot(p.astype(BF16), v, preferred_element_type=F32)

    def update_chunk(j):
        k, v, off = load_chunk(j)
        for mp, qm in enumerate(maps):
            s = _dot_nt(qm, k) - rel_scr[...]
            m = m_scr[mp]
            m_new = jnp.maximum(m, jnp.max(s, axis=-1, keepdims=True) - off)
            alpha = jnp.exp2(m - m_new)
            p = jnp.exp2(s - jnp.tile(m_new + off, (1, lane_tiles)))
            m_scr[mp] = m_new
            l_scr[mp] = alpha * l_scr[mp] + lane_partial_sum(p)
            a_scr[mp] = alpha * a_scr[mp] + jnp.dot(p.astype(BF16), v, preferred_element_type=F32)

    def chunks_keep_max(js):
        loaded = [load_chunk(j) for j in js]
        for mp, qm in enumerate(maps):
            m = m_scr[mp]
            l_new = a_new = None
            for k, v, off in loaded:
                p = jnp.exp2(_dot_nt(qm, k) - rel_scr[...] - jnp.tile(m + off, (1, lane_tiles)))
                ls = lane_partial_sum(p)
                av = jnp.dot(p.astype(BF16), v, preferred_element_type=F32)
                l_new = ls if l_new is None else l_new + ls
                a_new = av if a_new is None else a_new + av
            l_scr[mp] = l_scr[mp] + l_new
            a_scr[mp] = a_scr[mp] + a_new

    sel_r = lax.broadcasted_iota(jnp.int32, (LANES, LANES), 0)
    sel_c = lax.broadcasted_iota(jnp.int32, (LANES, LANES), 1)
    sel = jnp.where(sel_c == lax.shift_right_logical(sel_r, HEAD_DIM.bit_length() - 1), 1.0, 0.0).astype(BF16)

    def max_sq_norm(x):
        xf = x.astype(F32)
        n2 = jnp.dot((xf * xf).astype(BF16), sel, preferred_element_type=F32)
        return jnp.max(n2, axis=0, keepdims=True)

    @pl.when(i == 0)
    def _():
        def body(jj, best):
            kk = k_ref[pl.ds(pl.multiple_of(jj * t, t), t), :]
            return jnp.maximum(best, max_sq_norm(kk))
        kn2 = lax.fori_loop(0, k_ref.shape[0] // t, body, jnp.zeros((1, LANES), F32), unroll=True)
        kn_scr[...] = jnp.broadcast_to(kn2, kn_scr.shape)
        r = lax.broadcasted_iota(jnp.int32, (t, t), 0)
        c = lax.broadcasted_iota(jnp.int32, (t, t), 1)
        rel = (r - c).astype(F32) * slope
        rel_scr[...] = rel
        diag_scr[...] = jnp.where(c <= r, rel, jnp.inf)

    bound = jnp.sqrt(max_sq_norm(q_ref[...]) * kn_scr[0:1, :]) * NORM_BOUND_MARGIN + NORM_BOUND_SLACK

    def slack():
        gaps = [bound[:, mp:mp + 1] - jnp.min(m_scr[mp], axis=0, keepdims=True)[:, 0:1]
                for mp in range(len(maps))]
        return functools.reduce(jnp.maximum, gaps)[0, 0]

    diagonal_chunk()

    def live(j, slk):
        nearest = ((i - j - 1) * t + 1).astype(F32)
        return jnp.logical_and(j >= 0, slk - slope * nearest >= DIFF_DEAD_LOG2)

    def cond(st):
        return live(*st)

    def keep_two(j, slk):
        chunks_keep_max((j, j - 1))
        return j - 2, slk

    def keep_one(j, slk):
        chunks_keep_max((j,))
        return j - 1, slk

    def keep_four(j, slk):
        chunks_keep_max((j, j - 1, j - 2, j - 3))
        return j - 4, slk

    def keep_few(j, slk):
        return lax.cond(live(j - 1, slk), keep_two, keep_one, j, slk)

    def keep_max(j, slk):
        return lax.cond(live(j - 3, slk), keep_four, keep_few, j, slk)

    def update_max(j, slk):
        update_chunk(j)
        return j - 1, slack()

    def body(st):
        j, slk = st
        return lax.cond(slk <= DIFF_EXP_HEADROOM, keep_max, update_max, j, slk)

    lax.while_loop(cond, body, (i - 1, slack()))

    lam = lam_ref[0:1, :]
    l0 = jnp.sum(l_scr[0], axis=-1, keepdims=True)
    l1 = jnp.sum(l_scr[1], axis=-1, keepdims=True)
    o = a_scr[0] / l0 - lam * (a_scr[1] / l1)
    o_ref[...] = (_rms(o, subln_ref[...]) * out_scale).astype(o_ref.dtype)


def _diff_attn(proj, lam, subln, lambda_init):
    bsz, s, _ = proj.shape
    t = DIFF_T
    nh = DIFF_HEADS
    slopes = tuple(2.0 ** (-8.0 * (h + 1) / nh) for h in range(nh))
    return pl.pallas_call(
        functools.partial(_diff_attn_kernel, t=t, slopes=slopes, out_scale=1.0 - lambda_init),
        grid=(bsz, nh, s // t),
        in_specs=[
            pl.BlockSpec((None, t, LANES), lambda b, h, i: (b, i, h)),
            pl.BlockSpec((None, s, LANES), lambda b, h, i: (b, 0, nh + h)),
            pl.BlockSpec((None, s, LANES), lambda b, h, i: (b, 0, 2 * nh + h)),
            pl.BlockSpec((8, LANES), lambda b, h, i: (0, 0)),
            pl.BlockSpec((1, LANES), lambda b, h, i: (0, 0)),
        ],
        out_specs=pl.BlockSpec((None, t, LANES), lambda b, h, i: (b, i, h)),
        out_shape=jax.ShapeDtypeStruct((bsz, s, nh * LANES), BF16),
        scratch_shapes=[pltpu.VMEM((8, LANES), F32)] + [pltpu.VMEM((t, t), F32)] * 2
                       + [pltpu.VMEM((2, t, LANES), F32)] * 3,
        compiler_params=_cparams(("arbitrary", "arbitrary", "arbitrary")),
        name="diff_attn",
    )(proj, proj, proj, lam, subln.reshape(1, LANES))


def _sb_attn_kernel(q_ref, k_ref, v_ref, o_ref, *, t):
    i = pl.program_id(2)
    pair_cols = [slice(p * LANES, (p + 1) * LANES) for p in range(q_ref.shape[1] // LANES)]
    queries = [_split_heads(q_ref[:, cols]) for cols in pair_cols]
    n_heads = 2 * len(pair_cols)
    r = lax.broadcasted_iota(jnp.int32, (t, t), 0)
    c = lax.broadcasted_iota(jnp.int32, (t, t), 1)
    strict = c < r
    after = (r > c).astype(BF16)
    after2 = jnp.concatenate([after, after], axis=0)

    def prefix(j, *, diagonal, valid=None):
        start = pl.multiple_of(j * t, t)
        values, zs = [], []
        for cols, pair in zip(pair_cols, queries):
            k = k_ref[pl.ds(start, t), cols]
            v = v_ref[pl.ds(start, t), cols]
            if valid is not None:
                v = jnp.where(valid, v, jnp.zeros_like(v))
            values.extend(_split_heads(v))
            zs.extend(_dot_nt(qm, k) for qm in pair)
        log_betas, splits, firsts = [], [], []
        for z in zs:
            soft = jnp.log2(1.0 + jnp.exp2(-jnp.abs(z)))
            log_beta = jnp.minimum(z, 0.0) - soft
            x = log_beta - z
            if diagonal:
                x = jnp.where(strict, x, 0.0)
            xh = x.astype(BF16)
            xl = (x - xh.astype(F32)).astype(BF16)
            log_betas.append(log_beta)
            splits.append(jnp.concatenate([xh, xl], axis=1))
            firsts.append(x[:, 0:1])
        inners = [jnp.dot(sp, after2, preferred_element_type=F32) for sp in splits]
        parts = []
        for log_beta, inner, first in zip(log_betas, inners, firsts):
            total = inner[:, 0:1] + first
            if valid is not None:
                total = jnp.where(valid, total, 0.0)
            parts.append((log_beta, inner, total))
        return parts, values

    def finish(parts, values, carries, *, diagonal):
        atts = []
        for (log_beta, inner, _), carry in zip(parts, carries):
            att = jnp.exp2(log_beta + (inner + carry))
            if diagonal:
                att = jnp.where(strict, att, 0.0)
            atts.append(att.astype(BF16))
        outs = [jnp.dot(att, vm, preferred_element_type=F32) for att, vm in zip(atts, values)]
        new_carries = [carry + total for (_, _, total), carry in zip(parts, carries)]
        return new_carries, [outs[h] + outs[h + 1] for h in range(0, n_heads, 2)]

    has_prev = i > 0
    zero_col = jnp.zeros((t, 1), F32)
    diag_parts, diag_values = prefix(i, diagonal=True)
    prev_parts, prev_values = prefix(jnp.maximum(i - 1, 0), diagonal=False, valid=has_prev)
    carries, accs = finish(diag_parts, diag_values, [zero_col] * n_heads, diagonal=True)
    carries, accs_prev = finish(prev_parts, prev_values, carries, diagonal=False)
    accs = [a + b for a, b in zip(accs, accs_prev)]

    def alive(carries):
        return jnp.max(functools.reduce(jnp.maximum, carries)) > SB_DEAD_LOG2

    def cond(state):
        return jnp.logical_and(state[0] >= 0, state[-1])

    def body(state):
        j, carries, accs, _ = state
        parts, values = prefix(j, diagonal=False)
        carries, outs = finish(parts, values, carries, diagonal=False)
        return j - 1, carries, [a + b for a, b in zip(accs, outs)], alive(carries)

    _, _, accs, _ = lax.while_loop(cond, body, (i - 2, carries, accs, alive(carries)))
    for cols, acc in zip(pair_cols, accs):
        o_ref[:, cols] = acc


def _sb_attn(proj):
    bsz, s, _ = proj.shape
    t = SB_T
    pairs = SB_HEADS * HEAD_DIM // LANES
    width = SB_PAIRS_PER_STEP * LANES
    groups = pairs // SB_PAIRS_PER_STEP
    base = 3 * DIFF_HEADS // SB_PAIRS_PER_STEP
    whole_seq = lambda index_map: pl.BlockSpec((None, s, width), index_map)
    return pl.pallas_call(
        functools.partial(_sb_attn_kernel, t=t),
        grid=(bsz, groups, s // t),
        in_specs=[
            pl.BlockSpec((None, t, width), lambda b, g, i: (b, i, base + g)),
            whole_seq(lambda b, g, i: (b, 0, base + groups + g)),
            whole_seq(lambda b, g, i: (b, 0, base + 2 * groups + g)),
        ],
        out_specs=pl.BlockSpec((None, t, width), lambda b, g, i: (b, i, g)),
        out_shape=jax.ShapeDtypeStruct((bsz, s, pairs * LANES), F32),
        compiler_params=_cparams(("parallel", "parallel", "arbitrary")),
        name="sb_attn",
    )(proj, proj, proj)


def _mix_out_ffn_kernel(x_ref, od_ref, os_ref, mgt_ref, beta_ref, mgpost_ref, wod_ref, wos_ref,
                        sh_ref, sc_ref, gt_ref, gpre_ref, gpost_ref, wg_ref, wu_ref, wd_ref,
                        o_ref, x_scr, h_scr, a_scr, y_scr, *, tf):
    osn = _rms(os_ref[...], beta_ref[...]).astype(BF16)
    y = (jnp.dot(od_ref[...], wod_ref[...], preferred_element_type=F32)
         + jnp.dot(osn, wos_ref[...], preferred_element_type=F32))
    x_scr[...] = x_ref[...] + MIX_RES_WEIGHT * mgt_ref[...] * _rms(y, mgpost_ref[...])
    o_ref[...] = _swiglu_sublayer(x_scr, sh_ref[...], sc_ref[...], gt_ref[...], gpre_ref[...], gpost_ref[...],
                                  wg_ref, wu_ref, wd_ref, h_scr, a_scr, y_scr, tf)


def _mix_out_ffn(x, o_d, o_s, mod4, k_gate, sb_beta, mix_g_post, w_out, k0, g_pre, g_post, w_gate, w_up, w_down):
    bsz, s, d = x.shape
    dd, ds = o_d.shape[-1], o_s.shape[-1]
    d_ff = w_gate.shape[1]
    tm, tf = FFN_TM, FFN_TF
    w_bf = w_out.astype(BF16)
    tile = lambda n: pl.BlockSpec((None, tm, n), lambda b, i: (b, i, 0))
    mod_spec = lambda k: pl.BlockSpec((None, None, 1, d), lambda b, i: (b, k, 0, 0))
    whole = lambda r, c: _resident((r, c), lambda b, i: (0, 0))
    return pl.pallas_call(
        functools.partial(_mix_out_ffn_kernel, tf=tf),
        grid=(bsz, s // tm),
        in_specs=[
            tile(d), tile(dd), tile(ds),
            mod_spec(k_gate), whole(1, ds), whole(1, d), whole(dd, d), whole(ds, d),
            mod_spec(k0), mod_spec(k0 + 1), mod_spec(k0 + 2), whole(1, d), whole(1, d),
            whole(d, d_ff), whole(d, d_ff), whole(d_ff, d),
        ],
        out_specs=tile(d),
        out_shape=jax.ShapeDtypeStruct(x.shape, F32),
        scratch_shapes=[pltpu.VMEM((tm, d), F32), pltpu.VMEM((tm, d), BF16), pltpu.VMEM((tm, d_ff), BF16),
                        pltpu.VMEM((tm, d), F32)],
        compiler_params=_cparams(("parallel", "parallel")),
        name="mix_out_ffn",
    )(x, o_d, o_s, mod4, sb_beta.reshape(1, ds), mix_g_post.reshape(1, d), w_bf[:dd], w_bf[dd:],
      mod4, mod4, mod4, g_pre.reshape(1, d), g_post.reshape(1, d),
      w_gate.astype(BF16), w_up.astype(BF16), w_down.astype(BF16))


def kernel(x, c, w_ada, b_ada, ffn1_g_pre, ffn1_g_post, ffn1_w_gate, ffn1_w_up, ffn1_w_down, mix_g_pre, mix_g_post, w_in, w_out, lam_q1, lam_k1, lam_q2, lam_k2, diff_subln, sb_beta, ffn2_g_pre, ffn2_g_post, ffn2_w_gate, ffn2_w_up, ffn2_w_down):
    bsz, _, d = x.shape
    depth = w_ada.shape[0]
    for l in range(depth):
        lambda_init = 0.8 - 0.6 * math.exp(-0.3 * l)
        lam_params = jnp.stack([lam_q1[l], lam_k1[l], lam_q2[l], lam_k2[l]])
        mod, lam = _adaln(c, w_ada[l], b_ada[l], lam_params, lambda_init)
        mod4 = mod.reshape(bsz, N_MOD, 1, d)
        x = _ffn(x, mod4, 0, ffn1_g_pre[l], ffn1_g_post[l], ffn1_w_gate[l], ffn1_w_up[l], ffn1_w_down[l])
        proj = _mix_in(x, mod4, 3, mix_g_pre[l], w_in[l])
        o_d = _diff_attn(proj, lam, diff_subln[l], lambda_init)
        o_s = _sb_attn(proj)
        x = _mix_out_ffn(x, o_d, o_s, mod4, 5, sb_beta[l], mix_g_post[l], w_out[l],
                         6, ffn2_g_pre[l], ffn2_g_post[l], ffn2_w_gate[l], ffn2_w_up[l], ffn2_w_down[l])
    return x
```

```python
import functools
import math

import jax
import jax.numpy as jnp
from jax import lax
from jax.experimental import pallas as pl
from jax.experimental.pallas import tpu as pltpu

F32 = jnp.float32
BF16 = jnp.bfloat16

HEAD_DIM = 64
DIFF_HEADS = 4
SB_HEADS = 8
N_MOD = 9
RMS_EPS = 1e-6
FFN_RES_WEIGHT = 0.5
MIX_RES_WEIGHT = 1.0

LANES = 128
VMEM_LIMIT_BYTES = 56 * 1024 * 1024

ADA_TN = 1024
FFN_TM = 512
FFN_TF = 256
DIFF_T = 512
SB_T = 256
SB_PAIRS_PER_STEP = 4

LOG2E = math.log2(math.e)
DIFF_Q_SCALE = LOG2E * HEAD_DIM ** -0.5
SB_Q_SCALE = LOG2E * HEAD_DIM ** -0.5

SB_DEAD_LOG2 = -160.0
DIFF_DEAD_LOG2 = -150.0
DIFF_EXP_HEADROOM = 100.0
NORM_BOUND_MARGIN = 1.01
NORM_BOUND_SLACK = 1.0


def _cparams(semantics):
    return pltpu.CompilerParams(dimension_semantics=semantics, vmem_limit_bytes=VMEM_LIMIT_BYTES)


def _resident(block_shape, index_map):
    return pl.BlockSpec(block_shape, index_map, pipeline_mode=pl.Buffered(1))


def _sigmoid(x):
    return 1.0 / (1.0 + jnp.exp(-x))


def _rms(x, gain):
    ms = jnp.mean(x * x, axis=-1, keepdims=True)
    return x * lax.rsqrt(ms + RMS_EPS) * gain


def _adaln_kernel(c_ref, w_ref, b_ref, lamp_ref, mod_ref, lam_ref, *, lambda_init):
    c = c_ref[...]
    s = c * _sigmoid(c)
    mod_ref[...] = jnp.dot(s, w_ref[...], preferred_element_type=F32,
                           precision=lax.Precision.HIGHEST) + b_ref[...]
    lp = lamp_ref[...]
    d1 = jnp.sum(lp[0:1] * lp[1:2], axis=-1, keepdims=True)
    d2 = jnp.sum(lp[2:3] * lp[3:4], axis=-1, keepdims=True)
    lam = jnp.exp(d1) - jnp.exp(d2) + lambda_init
    lam_ref[...] = jnp.broadcast_to(lam, lam_ref.shape)


def _adaln(c, w_ada, b_ada, lam_params, lambda_init):
    bsz, d = c.shape
    n = w_ada.shape[1]
    tn = ADA_TN
    return pl.pallas_call(
        functools.partial(_adaln_kernel, lambda_init=lambda_init),
        grid=(n // tn,),
        in_specs=[
            pl.BlockSpec((bsz, d), lambda j: (0, 0)),
            pl.BlockSpec((d, tn), lambda j: (0, j)),
            pl.BlockSpec((1, tn), lambda j: (0, j)),
            pl.BlockSpec(lam_params.shape, lambda j: (0, 0)),
        ],
        out_specs=[
            pl.BlockSpec((bsz, tn), lambda j: (0, j)),
            pl.BlockSpec((8, LANES), lambda j: (0, 0)),
        ],
        out_shape=[
            jax.ShapeDtypeStruct((bsz, n), F32),
            jax.ShapeDtypeStruct((8, LANES), F32),
        ],
        compiler_params=_cparams(("arbitrary",)),
        name="adaln",
    )(c, w_ada, b_ada.reshape(1, n), lam_params)


def _modulated_norm(x, g_pre, scale, shift):
    return _rms(x, g_pre) * (1.0 + scale) + shift


def _swiglu_sublayer(x_ref, sh, sc, gt, g_pre, g_post, wg_ref, wu_ref, wd_ref, h_scr, a_scr, y_scr, tf):
    x = x_ref[...]
    inv = lax.rsqrt(jnp.mean(x * x, axis=-1, keepdims=True) + RMS_EPS)
    h_scr[...] = (x * inv * (g_pre * (1.0 + sc)) + sh).astype(BF16)
    d_ff = wg_ref.shape[1]
    for j in range(d_ff // tf):
        cols = slice(j * tf, (j + 1) * tf)
        h = h_scr[...]
        g = jnp.dot(h, wg_ref[:, cols], preferred_element_type=F32)
        u = jnp.dot(h, wu_ref[:, cols], preferred_element_type=F32)
        a_scr[:, cols] = (g * _sigmoid(g) * u).astype(BF16)
    d = wd_ref.shape[1]
    sq = None
    for n in range(d // tf):
        cols = slice(n * tf, (n + 1) * tf)
        y = jnp.dot(a_scr[...], wd_ref[:, cols], preferred_element_type=F32)
        y_scr[:, cols] = y
        part = jnp.sum(y * y, axis=-1, keepdims=True)
        sq = part if sq is None else sq + part
    inv = lax.rsqrt(sq * (1.0 / d) + RMS_EPS)
    return x_ref[...] + y_scr[...] * inv * (FFN_RES_WEIGHT * gt * g_post)


def _ffn_kernel(x_ref, sh_ref, sc_ref, gt_ref, gpre_ref, gpost_ref, wg_ref, wu_ref, wd_ref,
                o_ref, h_scr, a_scr, y_scr, *, tf):
    o_ref[...] = _swiglu_sublayer(x_ref, sh_ref[...], sc_ref[...], gt_ref[...], gpre_ref[...], gpost_ref[...],
                                  wg_ref, wu_ref, wd_ref, h_scr, a_scr, y_scr, tf)


def _ffn(x, mod4, k0, g_pre, g_post, w_gate, w_up, w_down):
    bsz, s, d = x.shape
    d_ff = w_gate.shape[1]
    tm, tf = FFN_TM, FFN_TF
    mod_spec = lambda k: pl.BlockSpec((None, None, 1, d), lambda b, i: (b, k, 0, 0))
    row = lambda: _resident((1, d), lambda b, i: (0, 0))
    return pl.pallas_call(
        functools.partial(_ffn_kernel, tf=tf),
        grid=(bsz, s // tm),
        in_specs=[
            pl.BlockSpec((None, tm, d), lambda b, i: (b, i, 0)),
            mod_spec(k0), mod_spec(k0 + 1), mod_spec(k0 + 2),
            row(), row(),
            _resident((d, d_ff), lambda b, i: (0, 0)),
            _resident((d, d_ff), lambda b, i: (0, 0)),
            _resident((d_ff, d), lambda b, i: (0, 0)),
        ],
        out_specs=pl.BlockSpec((None, tm, d), lambda b, i: (b, i, 0)),
        out_shape=jax.ShapeDtypeStruct(x.shape, F32),
        scratch_shapes=[pltpu.VMEM((tm, d), BF16), pltpu.VMEM((tm, d_ff), BF16), pltpu.VMEM((tm, d), F32)],
        compiler_params=_cparams(("parallel", "parallel")),
        name="ffn",
    )(x, mod4, mod4, mod4, g_pre.reshape(1, d), g_post.reshape(1, d),
      w_gate.astype(BF16), w_up.astype(BF16), w_down.astype(BF16))


def _mix_in_kernel(x_ref, sh_ref, sc_ref, gpre_ref, w_ref, o_ref, h_scr, *, group, group_scale):
    h_scr[...] = _modulated_norm(x_ref[...], gpre_ref[...], sc_ref[...], sh_ref[...]).astype(BF16)
    for g in range(w_ref.shape[1] // group):
        cols = slice(g * group, (g + 1) * group)
        p = jnp.dot(h_scr[...], w_ref[:, cols], preferred_element_type=F32)
        if g in group_scale:
            p = p * group_scale[g]
        o_ref[:, cols] = p.astype(BF16)


def _mix_in(x, mod4, k0, g_pre, w_in):
    bsz, s, d = x.shape
    n = w_in.shape[1]
    group = n // 6
    tm = FFN_TM
    mod_spec = lambda k: pl.BlockSpec((None, None, 1, d), lambda b, i: (b, k, 0, 0))
    return pl.pallas_call(
        functools.partial(_mix_in_kernel, group=group,
                          group_scale={0: DIFF_Q_SCALE, 3: SB_Q_SCALE}),
        grid=(bsz, s // tm),
        in_specs=[
            pl.BlockSpec((None, tm, d), lambda b, i: (b, i, 0)),
            mod_spec(k0), mod_spec(k0 + 1),
            _resident((1, d), lambda b, i: (0, 0)),
            _resident((d, n), lambda b, i: (0, 0)),
        ],
        out_specs=pl.BlockSpec((None, tm, n), lambda b, i: (b, i, 0)),
        out_shape=jax.ShapeDtypeStruct((bsz, s, n), BF16),
        scratch_shapes=[pltpu.VMEM((tm, d), BF16)],
        compiler_params=_cparams(("parallel", "parallel")),
        name="mix_in",
    )(x, mod4, mod4, g_pre.reshape(1, d), w_in.astype(BF16))


def _dot_nt(a, b):
    return lax.dot_general(a, b, (((1,), (1,)), ((), ())), preferred_element_type=F32)


def _split_heads(x):
    lane = lax.broadcasted_iota(jnp.int32, x.shape, 1)
    zero = jnp.zeros_like(x)
    return jnp.where(lane < HEAD_DIM, x, zero), jnp.where(lane >= HEAD_DIM, x, zero)


def _diff_attn_kernel(q_ref, k_ref, v_ref, lam_ref, subln_ref, o_ref, kn_scr, rel_scr, diag_scr,
                      m_scr, l_scr, a_scr, *, t, slopes, out_scale):
    head = pl.program_id(1)
    i = pl.program_id(2)
    slope = jnp.float32(slopes[-1])
    for hh in range(len(slopes) - 2, -1, -1):
        slope = jnp.where(head == hh, jnp.float32(slopes[hh]), slope)
    slope = slope * LOG2E

    maps = _split_heads(q_ref[...])
    lane_tiles = t // LANES

    def lane_partial_sum(p):
        parts = [p[:, n * LANES:(n + 1) * LANES] for n in range(p.shape[1] // LANES)]
        return functools.reduce(lambda a, b: a + b, parts)

    def load_chunk(j):
        start = pl.multiple_of(j * t, t)
        off = slope * ((i - j) * t).astype(F32)
        return k_ref[pl.ds(start, t), :], v_ref[pl.ds(start, t), :], off

    def diagonal_chunk():
        k, v, _ = load_chunk(i)
        for mp, qm in enumerate(maps):
            s = _dot_nt(qm, k) - diag_scr[...]
            m_new = jnp.broadcast_to(jnp.max(s, axis=-1, keepdims=True), (t, LANES))
            p = jnp.exp2(s - jnp.tile(m_new, (1, lane_tiles)))
            m_scr[mp] = m_new
            l_scr[mp] = lane_partial_sum(p)
            a_scr[mp] = jnp.dot(p.astype(BF16), v, preferred_element_type=F32)

    def update_chunk(j):
        k, v, off = load_chunk(j)
        for mp, qm in enumerate(maps):
            s = _dot_nt(qm, k) - rel_scr[...]
            m = m_scr[mp]
            m_new = jnp.maximum(m, jnp.max(s, axis=-1, keepdims=True) - off)
            alpha = jnp.exp2(m - m_new)
            p = jnp.exp2(s - jnp.tile(m_new + off, (1, lane_tiles)))
            m_scr[mp] = m_new
            l_scr[mp] = alpha * l_scr[mp] + lane_partial_sum(p)
            a_scr[mp] = alpha * a_scr[mp] + jnp.dot(p.astype(BF16), v, preferred_element_type=F32)

    def chunks_keep_max(js):
        loaded = [load_chunk(j) for j in js]
        for mp, qm in enumerate(maps):
            m = m_scr[mp]
            l_new = a_new = None
            for k, v, off in loaded:
                p = jnp.exp2(_dot_nt(qm, k) - rel_scr[...] - jnp.tile(m + off, (1, lane_tiles)))
                ls = lane_partial_sum(p)
                av = jnp.dot(p.astype(BF16), v, preferred_element_type=F32)
                l_new = ls if l_new is None else l_new + ls
                a_new = av if a_new is None else a_new + av
            l_scr[mp] = l_scr[mp] + l_new
            a_scr[mp] = a_scr[mp] + a_new

    sel_r = lax.broadcasted_iota(jnp.int32, (LANES, LANES), 0)
    sel_c = lax.broadcasted_iota(jnp.int32, (LANES, LANES), 1)
    sel = jnp.where(sel_c == lax.shift_right_logical(sel_r, HEAD_DIM.bit_length() - 1), 1.0, 0.0).astype(BF16)

    def max_sq_norm(x):
        xf = x.astype(F32)
        n2 = jnp.dot((xf * xf).astype(BF16), sel, preferred_element_type=F32)
        return jnp.max(n2, axis=0, keepdims=True)

    @pl.when(i == 0)
    def _():
        def body(jj, best):
            kk = k_ref[pl.ds(pl.multiple_of(jj * t, t), t), :]
            return jnp.maximum(best, max_sq_norm(kk))
        kn2 = lax.fori_loop(0, k_ref.shape[0] // t, body, jnp.zeros((1, LANES), F32), unroll=True)
        kn_scr[...] = jnp.broadcast_to(kn2, kn_scr.shape)
        r = lax.broadcasted_iota(jnp.int32, (t, t), 0)
        c = lax.broadcasted_iota(jnp.int32, (t, t), 1)
        rel = (r - c).astype(F32) * slope
        rel_scr[...] = rel
        diag_scr[...] = jnp.where(c <= r, rel, jnp.inf)

    bound = jnp.sqrt(max_sq_norm(q_ref[...]) * kn_scr[0:1, :]) * NORM_BOUND_MARGIN + NORM_BOUND_SLACK

    def slack():
        gaps = [bound[:, mp:mp + 1] - jnp.min(m_scr[mp], axis=0, keepdims=True)[:, 0:1]
                for mp in range(len(maps))]
        return functools.reduce(jnp.maximum, gaps)[0, 0]

    diagonal_chunk()

    def live(j, slk):
        nearest = ((i - j - 1) * t + 1).astype(F32)
        return jnp.logical_and(j >= 0, slk - slope * nearest >= DIFF_DEAD_LOG2)

    def cond(st):
        return live(*st)

    def keep_two(j, slk):
        chunks_keep_max((j, j - 1))
        return j - 2, slk

    def keep_one(j, slk):
        chunks_keep_max((j,))
        return j - 1, slk

    def keep_four(j, slk):
        chunks_keep_max((j, j - 1, j - 2, j - 3))
        return j - 4, slk

    def keep_few(j, slk):
        return lax.cond(live(j - 1, slk), keep_two, keep_one, j, slk)

    def keep_max(j, slk):
        return lax.cond(live(j - 3, slk), keep_four, keep_few, j, slk)

    def update_max(j, slk):
        update_chunk(j)
        return j - 1, slack()

    def body(st):
        j, slk = st
        return lax.cond(slk <= DIFF_EXP_HEADROOM, keep_max, update_max, j, slk)

    lax.while_loop(cond, body, (i - 1, slack()))

    lam = lam_ref[0:1, :]
    l0 = jnp.sum(l_scr[0], axis=-1, keepdims=True)
    l1 = jnp.sum(l_scr[1], axis=-1, keepdims=True)
    o = a_scr[0] / l0 - lam * (a_scr[1] / l1)
    o_ref[...] = (_rms(o, subln_ref[...]) * out_scale).astype(o_ref.dtype)


def _diff_attn(proj, lam, subln, lambda_init):
    bsz, s, _ = proj.shape
    t = DIFF_T
    nh = DIFF_HEADS
    slopes = tuple(2.0 ** (-8.0 * (h + 1) / nh) for h in range(nh))
    return pl.pallas_call(
        functools.partial(_diff_attn_kernel, t=t, slopes=slopes, out_scale=1.0 - lambda_init),
        grid=(bsz, nh, s // t),
        in_specs=[
            pl.BlockSpec((None, t, LANES), lambda b, h, i: (b, i, h)),
            pl.BlockSpec((None, s, LANES), lambda b, h, i: (b, 0, nh + h)),
            pl.BlockSpec((None, s, LANES), lambda b, h, i: (b, 0, 2 * nh + h)),
            pl.BlockSpec((8, LANES), lambda b, h, i: (0, 0)),
            pl.BlockSpec((1, LANES), lambda b, h, i: (0, 0)),
        ],
        out_specs=pl.BlockSpec((None, t, LANES), lambda b, h, i: (b, i, h)),
        out_shape=jax.ShapeDtypeStruct((bsz, s, nh * LANES), BF16),
        scratch_shapes=[pltpu.VMEM((8, LANES), F32)] + [pltpu.VMEM((t, t), F32)] * 2
                       + [pltpu.VMEM((2, t, LANES), F32)] * 3,
        compiler_params=_cparams(("arbitrary", "arbitrary", "arbitrary")),
        name="diff_attn",
    )(proj, proj, proj, lam, subln.reshape(1, LANES))


def _sb_attn_kernel(q_ref, k_ref, v_ref, o_ref, *, t):
    i = pl.program_id(2)
    pair_cols = [slice(p * LANES, (p + 1) * LANES) for p in range(q_ref.shape[1] // LANES)]
    queries = [_split_heads(q_ref[:, cols]) for cols in pair_cols]
    n_heads = 2 * len(pair_cols)
    r = lax.broadcasted_iota(jnp.int32, (t, t), 0)
    c = lax.broadcasted_iota(jnp.int32, (t, t), 1)
    strict = c < r
    after = (r > c).astype(BF16)
    after2 = jnp.concatenate([after, after], axis=0)

    def prefix(j, *, diagonal, valid=None):
        start = pl.multiple_of(j * t, t)
        values, zs = [], []
        for cols, pair in zip(pair_cols, queries):
            k = k_ref[pl.ds(start, t), cols]
            v = v_ref[pl.ds(start, t), cols]
            if valid is not None:
                v = jnp.where(valid, v, jnp.zeros_like(v))
            values.extend(_split_heads(v))
            zs.extend(_dot_nt(qm, k) for qm in pair)
        log_betas, splits, firsts = [], [], []
        for z in zs:
            soft = jnp.log2(1.0 + jnp.exp2(-jnp.abs(z)))
            log_beta = jnp.minimum(z, 0.0) - soft
            x = log_beta - z
            if diagonal:
                x = jnp.where(strict, x, 0.0)
            xh = x.astype(BF16)
            xl = (x - xh.astype(F32)).astype(BF16)
            log_betas.append(log_beta)
            splits.append(jnp.concatenate([xh, xl], axis=1))
            firsts.append(x[:, 0:1])
        inners = [jnp.dot(sp, after2, preferred_element_type=F32) for sp in splits]
        parts = []
        for log_beta, inner, first in zip(log_betas, inners, firsts):
            total = inner[:, 0:1] + first
            if valid is not None:
                total = jnp.where(valid, total, 0.0)
            parts.append((log_beta, inner, total))
        return parts, values

    def finish(parts, values, carries, *, diagonal):
        atts = []
        for (log_beta, inner, _), carry in zip(parts, carries):
            att = jnp.exp2(log_beta + (inner + carry))
            if diagonal:
                att = jnp.where(strict, att, 0.0)
            atts.append(att.astype(BF16))
        outs = [jnp.dot(att, vm, preferred_element_type=F32) for att, vm in zip(atts, values)]
        new_carries = [carry + total for (_, _, total), carry in zip(parts, carries)]
        return new_carries, [outs[h] + outs[h + 1] for h in range(0, n_heads, 2)]

    has_prev = i > 0
    zero_col = jnp.zeros((t, 1), F32)
    diag_parts, diag_values = prefix(i, diagonal=True)
    prev_parts, prev_values = prefix(jnp.maximum(i - 1, 0), diagonal=False, valid=has_prev)
    carries, accs = finish(diag_parts, diag_values, [zero_col] * n_heads, diagonal=True)
    carries, accs_prev = finish(prev_parts, prev_values, carries, diagonal=False)
    accs = [a + b for a, b in zip(accs, accs_prev)]

    def alive(carries):
        return jnp.max(functools.reduce(jnp.maximum, carries)) > SB_DEAD_LOG2

    def cond(state):
        return jnp.logical_and(state[0] >= 0, state[-1])

    def body(state):
        j, carries, accs, _ = state
        parts, values = prefix(j, diagonal=False)
        carries, outs = finish(parts, values, carries, diagonal=False)
        return j - 1, carries, [a + b for a, b in zip(accs, outs)], alive(carries)

    _, _, accs, _ = lax.while_loop(cond, body, (i - 2, carries, accs, alive(carries)))
    for cols, acc in zip(pair_cols, accs):
        o_ref[:, cols] = acc


def _sb_attn(proj):
    bsz, s, _ = proj.shape
    t = SB_T
    pairs = SB_HEADS * HEAD_DIM // LANES
    width = SB_PAIRS_PER_STEP * LANES
    groups = pairs // SB_PAIRS_PER_STEP
    base = 3 * DIFF_HEADS // SB_PAIRS_PER_STEP
    whole_seq = lambda index_map: pl.BlockSpec((None, s, width), index_map)
    return pl.pallas_call(
        functools.partial(_sb_attn_kernel, t=t),
        grid=(bsz, groups, s // t),
        in_specs=[
            pl.BlockSpec((None, t, width), lambda b, g, i: (b, i, base + g)),
            whole_seq(lambda b, g, i: (b, 0, base + groups + g)),
            whole_seq(lambda b, g, i: (b, 0, base + 2 * groups + g)),
        ],
        out_specs=pl.BlockSpec((None, t, width), lambda b, g, i: (b, i, g)),
        out_shape=jax.ShapeDtypeStruct((bsz, s, pairs * LANES), F32),
        compiler_params=_cparams(("parallel", "parallel", "arbitrary")),
        name="sb_attn",
    )(proj, proj, proj)


def _mix_out_ffn_kernel(x_ref, od_ref, os_ref, mgt_ref, beta_ref, mgpost_ref, wod_ref, wos_ref,
                        sh_ref, sc_ref, gt_ref, gpre_ref, gpost_ref, wg_ref, wu_ref, wd_ref,
                        o_ref, x_scr, h_scr, a_scr, y_scr, *, tf):
    osn = _rms(os_ref[...], beta_ref[...]).astype(BF16)
    y = (jnp.dot(od_ref[...], wod_ref[...], preferred_element_type=F32)
         + jnp.dot(osn, wos_ref[...], preferred_element_type=F32))
    x_scr[...] = x_ref[...] + MIX_RES_WEIGHT * mgt_ref[...] * _rms(y, mgpost_ref[...])
    o_ref[...] = _swiglu_sublayer(x_scr, sh_ref[...], sc_ref[...], gt_ref[...], gpre_ref[...], gpost_ref[...],
                                  wg_ref, wu_ref, wd_ref, h_scr, a_scr, y_scr, tf)


def _mix_out_ffn(x, o_d, o_s, mod4, k_gate, sb_beta, mix_g_post, w_out, k0, g_pre, g_post, w_gate, w_up, w_down):
    bsz, s, d = x.shape
    dd, ds = o_d.shape[-1], o_s.shape[-1]
    d_ff = w_gate.shape[1]
    tm, tf = FFN_TM, FFN_TF
    w_bf = w_out.astype(BF16)
    tile = lambda n: pl.BlockSpec((None, tm, n), lambda b, i: (b, i, 0))
    mod_spec = lambda k: pl.BlockSpec((None, None, 1, d), lambda b, i: (b, k, 0, 0))
    whole = lambda r, c: _resident((r, c), lambda b, i: (0, 0))
    return pl.pallas_call(
        functools.partial(_mix_out_ffn_kernel, tf=tf),
        grid=(bsz, s // tm),
        in_specs=[
            tile(d), tile(dd), tile(ds),
            mod_spec(k_gate), whole(1, ds), whole(1, d), whole(dd, d), whole(ds, d),
            mod_spec(k0), mod_spec(k0 + 1), mod_spec(k0 + 2), whole(1, d), whole(1, d),
            whole(d, d_ff), whole(d, d_ff), whole(d_ff, d),
        ],
        out_specs=tile(d),
        out_shape=jax.ShapeDtypeStruct(x.shape, F32),
        scratch_shapes=[pltpu.VMEM((tm, d), F32), pltpu.VMEM((tm, d), BF16), pltpu.VMEM((tm, d_ff), BF16),
                        pltpu.VMEM((tm, d), F32)],
        compiler_params=_cparams(("parallel", "parallel")),
        name="mix_out_ffn",
    )(x, o_d, o_s, mod4, sb_beta.reshape(1, ds), mix_g_post.reshape(1, d), w_bf[:dd], w_bf[dd:],
      mod4, mod4, mod4, g_pre.reshape(1, d), g_post.reshape(1, d),
      w_gate.astype(BF16), w_up.astype(BF16), w_down.astype(BF16))


def kernel(x, c, w_ada, b_ada, ffn1_g_pre, ffn1_g_post, ffn1_w_gate, ffn1_w_up, ffn1_w_down, mix_g_pre, mix_g_post, w_in, w_out, lam_q1, lam_k1, lam_q2, lam_k2, diff_subln, sb_beta, ffn2_g_pre, ffn2_g_post, ffn2_w_gate, ffn2_w_up, ffn2_w_down):
    bsz, seq, d = x.shape
    depth = w_ada.shape[0]
    mix_width = DIFF_HEADS * 2 * HEAD_DIM
    assert seq % FFN_TM == 0 and seq % DIFF_T == 0 and seq % SB_T == 0, seq
    assert d % LANES == 0 and w_ada.shape[1:] == (d, N_MOD * d), (d, w_ada.shape)
    assert SB_HEADS * HEAD_DIM == mix_width and w_in.shape[1:] == (d, 6 * mix_width), w_in.shape
    assert w_out.shape[1:] == (2 * mix_width, d) and diff_subln.shape[1:] == (2 * HEAD_DIM,), w_out.shape
    assert (SB_HEADS * HEAD_DIM // LANES) % SB_PAIRS_PER_STEP == 0 and (3 * DIFF_HEADS) % SB_PAIRS_PER_STEP == 0
    assert ffn1_w_gate.shape[2] % FFN_TF == 0 and d % FFN_TF == 0, ffn1_w_gate.shape
    for l in range(depth):
        lambda_init = 0.8 - 0.6 * math.exp(-0.3 * l)
        lam_params = jnp.stack([lam_q1[l], lam_k1[l], lam_q2[l], lam_k2[l]])
        mod, lam = _adaln(c, w_ada[l], b_ada[l], lam_params, lambda_init)
        mod4 = mod.reshape(bsz, N_MOD, 1, d)
        x = _ffn(x, mod4, 0, ffn1_g_pre[l], ffn1_g_post[l], ffn1_w_gate[l], ffn1_w_up[l], ffn1_w_down[l])
        proj = _mix_in(x, mod4, 3, mix_g_pre[l], w_in[l])
        o_d = _diff_attn(proj, lam, diff_subln[l], lambda_init)
        o_s = _sb_attn(proj)
        x = _mix_out_ffn(x, o_d, o_s, mod4, 5, sb_beta[l], mix_g_post[l], w_out[l],
                         6, ffn2_g_pre[l], ffn2_g_post[l], ffn2_w_gate[l], ffn2_w_up[l], ffn2_w_down[l])
    return x
```
